```python
import math
import jax
import jax.numpy as jnp
from jax import lax
import numpy as np

D_MODEL = 2048
BATCH = 4
SEQ = 4096
DEPTH = 2

CTX_LEN = 256
GRID_W = 64
MIX_WIDTH = D_MODEL
N_GROUPS = 4
GROUP_WIDTH = MIX_WIDTH // N_GROUPS
HEAD_DIM = 128
GROUP_HEADS = GROUP_WIDTH // HEAD_DIM
GLA_HEADS = GROUP_HEADS
GLA_DK = HEAD_DIM // 2
GLA_GATE_RANK = 16
GLA_GATE_TAU = 16.0
GLA_CHUNK = 16
DN_HEADS = GROUP_HEADS
DN_CHUNK = 64
SHORT_CONV = 5
GQA_HEADS = GROUP_HEADS
GQA_KV_HEADS = 2
DIFF_HEADS = GROUP_HEADS
DIFF_DH = HEAD_DIM // 2
Q_BLOCK = 128
ROPE_THETA = 10000.0
N_EXPERTS = 16
EC_CAPACITY = 2
EXPERT_FF = D_MODEL // 2
RMS_EPS = 1e-6

IN_SPLITS = (
    ('gla_q', GLA_HEADS * GLA_DK),
    ('gla_k', GLA_HEADS * GLA_DK),
    ('gla_v', GLA_HEADS * HEAD_DIM),
    ('gla_g', GLA_HEADS * HEAD_DIM),
    ('gla_lr', 2 * GLA_GATE_RANK),
    ('dn_qkv', 3 * DN_HEADS * HEAD_DIM),
    ('dn_g', DN_HEADS * HEAD_DIM),
    ('dn_a', 2 * DN_HEADS),
    ('dn_b', 2 * DN_HEADS),
    ('gqa_q', GQA_HEADS * HEAD_DIM),
    ('gqa_kv', 2 * GQA_KV_HEADS * HEAD_DIM),
    ('diff_q', DIFF_HEADS * 2 * DIFF_DH),
    ('diff_k', DIFF_HEADS * 2 * DIFF_DH),
    ('diff_v', DIFF_HEADS * HEAD_DIM),
)

kernel_name = "hybrid_parallel_groups_ec_moe_dit"


def rms_norm(x, w, eps=RMS_EPS):
    xf = x.astype(jnp.float32)
    y = xf * lax.rsqrt(jnp.mean(xf * xf, axis=-1, keepdims=True) + eps)
    return y.astype(x.dtype) * w


def l2_norm(x, eps=RMS_EPS):
    xf = x.astype(jnp.float32)
    return xf * lax.rsqrt(jnp.sum(xf * xf, axis=-1, keepdims=True) + eps)


def split_columns(z):
    sizes = [s for _, s in IN_SPLITS]
    cuts = [int(v) for v in np.cumsum(sizes)[:-1]]
    return dict(zip([n for n, _ in IN_SPLITS], jnp.split(z, cuts, axis=-1)))


def axial_rope(x, pos_r, pos_c):
    d = x.shape[-1]
    half = d // 2
    inv = ROPE_THETA ** (-jnp.arange(0, half, 2, dtype=jnp.float32) / half)

    def rot(xa, pos):
        ang = pos.astype(jnp.float32)[:, None] * inv
        ang = ang.reshape((ang.shape[0],) + (1,) * (x.ndim - 3) + (ang.shape[1],))
        cos, sin = jnp.cos(ang).astype(x.dtype), jnp.sin(ang).astype(x.dtype)
        x1, x2 = jnp.split(xa, 2, axis=-1)
        return jnp.concatenate([x1 * cos - x2 * sin, x1 * sin + x2 * cos], axis=-1)

    return jnp.concatenate([rot(x[..., :half], pos_r), rot(x[..., half:], pos_c)], axis=-1)


def short_conv(x, w):
    k, ch = w.shape
    return lax.conv_general_dilated(x, w[:, None, :], window_strides=(1,),
                                    padding=[(k // 2, k // 2)],
                                    dimension_numbers=('NWC', 'WIO', 'NWC'),
                                    feature_group_count=ch)


def gla_chunked(q, k, v, log_a, s0):
    out_dtype = v.dtype
    q, k, v, log_a = (t.astype(jnp.float32) for t in (q, k, v, log_a))
    B, T, H, dk = q.shape
    dv = v.shape[-1]
    n = T // GLA_CHUNK
    q, k, log_a = (t.reshape(B, n, GLA_CHUNK, H, dk) for t in (q, k, log_a))
    v = v.reshape(B, n, GLA_CHUNK, H, dv)
    b = jnp.cumsum(log_a, axis=2)
    idx = jnp.arange(GLA_CHUNK)
    incl = (idx[:, None] >= idx[None, :])[:, :, None, None]
    decay = jnp.exp(jnp.where(incl, b[:, :, :, None] - b[:, :, None], -jnp.inf))
    scores = jnp.einsum('bnihd,bnijhd,bnjhd->bnhij', q, decay, k)
    o_intra = jnp.einsum('bnhij,bnjhe->bnihe', scores, v)
    b_last = b[:, :, -1]
    qg = q * jnp.exp(b)
    kd = k * jnp.exp(b_last[:, :, None] - b)
    dl = jnp.exp(b_last)

    def step(S, inp):
        qg_c, kd_c, v_c, dl_c = inp
        o = jnp.einsum('bihd,bhde->bihe', qg_c, S)
        S = S * dl_c[..., None] + jnp.einsum('bjhd,bjhe->bhde', kd_c, v_c)
        return S, o

    xs = tuple(jnp.moveaxis(t, 1, 0) for t in (qg, kd, v, dl))
    S, o_inter = lax.scan(step, s0, xs)
    o = o_intra + jnp.moveaxis(o_inter, 0, 1)
    return o.reshape(B, T, H, dv).astype(out_dtype), S


def gated_delta_chunked(q, k, v, g, beta, s0):
    out_dtype = v.dtype
    q, k, v, g, beta = (t.astype(jnp.float32) for t in (q, k, v, g, beta))
    B, T, H, dk = q.shape
    dv = v.shape[-1]
    C = DN_CHUNK
    n = T // C
    q, k = (t.reshape(B, n, C, H, dk) for t in (q, k))
    v = v.reshape(B, n, C, H, dv)
    g, beta = (t.reshape(B, n, C, H) for t in (g, beta))
    gc = jnp.cumsum(g, axis=2)
    gh = jnp.moveaxis(gc, 3, 2)
    idx = jnp.arange(C)
    incl = idx[:, None] >= idx[None, :]
    strict = idx[:, None] > idx[None, :]
    decay = jnp.exp(jnp.where(incl, gh[..., :, None] - gh[..., None, :], -jnp.inf))
    kb = k * beta[..., None]
    lower = jnp.where(strict, jnp.einsum('bnihd,bnjhd->bnhij', kb, k) * decay, 0.0)
    eye = jnp.broadcast_to(jnp.eye(C, dtype=jnp.float32), lower.shape)
    tmat = lax.linalg.triangular_solve(lower + eye, eye, left_side=True, lower=True)
    u = jnp.einsum('bnhij,bnjhe->bnihe', tmat, v * beta[..., None])
    w = jnp.einsum('bnhij,bnjhd->bnihd', tmat, kb * jnp.exp(gc)[..., None])
    attn = jnp.einsum('bnihd,bnjhd->bnhij', q, k) * decay
    qg = q * jnp.exp(gc)[..., None]
    g_last = gc[:, :, -1]
    kd = k * jnp.exp(g_last[:, :, None] - gc)[..., None]
    dl = jnp.exp(g_last)

    def step(S, inp):
        u_c, w_c, qg_c, a_c, kd_c, dl_c = inp
        v_new = u_c - jnp.einsum('bihd,bhde->bihe', w_c, S)
        o = jnp.einsum('bihd,bhde->bihe', qg_c, S) + jnp.einsum('bhij,bjhe->bihe', a_c, v_new)
        S = S * dl_c[:, :, None, None] + jnp.einsum('bjhd,bjhe->bhde', kd_c, v_new)
        return S, o

    xs = tuple(jnp.moveaxis(t, 1, 0) for t in (u, w, qg, attn, kd, dl))
    S, o = lax.scan(step, s0, xs)
    return jnp.moveaxis(o, 0, 1).reshape(B, T, H, dv).astype(out_dtype), S


def run_direction(scan_fn, ctx_in, lat_in, s0, reverse):
    flip = (lambda t: jnp.flip(t, axis=1)) if reverse else (lambda t: t)
    o_ctx, s_ctx = scan_fn(*[flip(t) for t in ctx_in], s0)
    o_lat, _ = scan_fn(*[flip(t) for t in lat_in], s_ctx)
    return flip(o_ctx), flip(o_lat)


def blocked_queries(attend, q, *kv):
    B, T = q.shape[:2]
    qb = q.reshape((B, T // Q_BLOCK, Q_BLOCK) + q.shape[2:]).swapaxes(0, 1)
    ob = lax.map(lambda qi: attend(qi, *kv), qb)
    return ob.swapaxes(0, 1).reshape((B, T) + ob.shape[3:])


def gqa_attend(q, k, v):
    B, Tq, H, d = q.shape
    G = k.shape[2]
    qg = q.reshape(B, Tq, G, H // G, d)
    s = jnp.einsum('bqgrd,bkgd->bgrqk', qg, k).astype(jnp.float32) * d ** -0.5
    p = jax.nn.softmax(s, axis=-1).astype(v.dtype)
    return jnp.einsum('bgrqk,bkgd->bqgrd', p, v).reshape(B, Tq, H, d)


def diff_attend(q, k, v, lam):
    s = jnp.einsum('bqhcd,bkhcd->bhcqk', q, k).astype(jnp.float32) * DIFF_DH ** -0.5
    p = jax.nn.softmax(s, axis=-1)
    a = (p[:, :, 0] - lam * p[:, :, 1]).astype(v.dtype)
    return jnp.einsum('bhqk,bkhe->bqhe', a, v)


def gla_mixer(zc, zl, lp, need_ctx):
    def prep(z):
        B, T, _ = z['gla_q'].shape
        q = z['gla_q'].reshape(B, T, GLA_HEADS, GLA_DK) * GLA_DK ** -0.5
        k = z['gla_k'].reshape(B, T, GLA_HEADS, GLA_DK)
        v = z['gla_v'].reshape(B, T, GLA_HEADS, HEAD_DIM)
        lr = z['gla_lr'].reshape(B, T, 2, GLA_GATE_RANK)
        logit = jnp.einsum('btrl,rlk->btrk', lr, lp['gla_gate_w2']) + lp['gla_gate_b']
        log_a = jax.nn.log_sigmoid(logit.astype(jnp.float32)) / GLA_GATE_TAU
        return q, k, v, log_a.reshape(B, T, 2, GLA_HEADS, GLA_DK)

    qc, kc, vc, ac = prep(zc)
    ql, kl, vl, al = prep(zl)
    s0 = jnp.zeros((ql.shape[0], GLA_HEADS, GLA_DK, HEAD_DIM), jnp.float32)
    dirs = [run_direction(gla_chunked, (qc, kc, vc, ac[:, :, d]), (ql, kl, vl, al[:, :, d]), s0, d == 1)
            for d in range(2)]

    def post(o, z):
        B, T = o.shape[:2]
        return rms_norm(o, lp['gla_norm_w']).reshape(B, T, GROUP_WIDTH) * jax.nn.silu(z['gla_g'])

    o_lat = post(dirs[0][1] + dirs[1][1], zl)
    o_ctx = post(dirs[0][0] + dirs[1][0], zc) if need_ctx else None
    return o_ctx, o_lat


def deltanet_mixer(zc, zl, lp, need_ctx):
    def prep(z):
        B, T, _ = z['dn_qkv'].shape
        qkv = jax.nn.silu(short_conv(z['dn_qkv'], lp['dn_conv_w']))
        q, k, v = jnp.split(qkv, 3, axis=-1)
        q = l2_norm(q.reshape(B, T, DN_HEADS, HEAD_DIM)) * HEAD_DIM ** -0.5
        k = l2_norm(k.reshape(B, T, DN_HEADS, HEAD_DIM))
        v = v.reshape(B, T, DN_HEADS, HEAD_DIM)
        a = z['dn_a'].reshape(B, T, 2, DN_HEADS).astype(jnp.float32)
        g = -jnp.exp(lp['dn_a_log'].astype(jnp.float32)) * jax.nn.softplus(
            a + lp['dn_dt_bias'].astype(jnp.float32))
        beta = jax.nn.sigmoid(z['dn_b'].reshape(B, T, 2, DN_HEADS).astype(jnp.float32))
        return q, k, v, g, beta

    qc, kc, vc, gc, bc = prep(zc)
    ql, kl, vl, gl, bl = prep(zl)
    s0 = jnp.zeros((ql.shape[0], DN_HEADS, HEAD_DIM, HEAD_DIM), jnp.float32)
    dirs = [run_direction(gated_delta_chunked, (qc, kc, vc, gc[:, :, d], bc[:, :, d]),
                          (ql, kl, vl, gl[:, :, d], bl[:, :, d]), s0, d == 1)
            for d in range(2)]

    def post(o, z):
        B, T = o.shape[:2]
        return rms_norm(o, lp['dn_norm_w']).reshape(B, T, GROUP_WIDTH) * jax.nn.silu(z['dn_g'])

    o_lat = post(dirs[0][1] + dirs[1][1], zl)
    o_ctx = post(dirs[0][0] + dirs[1][0], zc) if need_ctx else None
    return o_ctx, o_lat


def gqa_mixer(zc, zl, lp, pos_r, pos_c, need_ctx):
    def prep(z):
        B, T, _ = z['gqa_q'].shape
        q = rms_norm(z['gqa_q'].reshape(B, T, GQA_HEADS, HEAD_DIM), lp['gqa_q_norm'])
        kv = z['gqa_kv'].reshape(B, T, 2, GQA_KV_HEADS, HEAD_DIM)
        return q, rms_norm(kv[:, :, 0], lp['gqa_k_norm']), kv[:, :, 1]

    qc, kc, vc = prep(zc)
    ql, kl, vl = prep(zl)
    ql = axial_rope(ql, pos_r, pos_c)
    kl = axial_rope(kl, pos_r, pos_c)
    k_all = jnp.concatenate([kc, kl], axis=1)
    v_all = jnp.concatenate([vc, vl], axis=1)
    o_lat = blocked_queries(gqa_attend, ql, k_all, v_all)
    B, N = o_lat.shape[:2]
    o_lat = o_lat.reshape(B, N, GROUP_WIDTH)
    o_ctx = gqa_attend(qc, kc, vc).reshape(B, qc.shape[1], GROUP_WIDTH) if need_ctx else None
    return o_ctx, o_lat


def diff_mixer(zc, zl, lp, lam_init, pos_r, pos_c, need_ctx):
    def prep(z):
        B, T, _ = z['diff_q'].shape
        q = z['diff_q'].reshape(B, T, DIFF_HEADS, 2, DIFF_DH)
        k = z['diff_k'].reshape(B, T, DIFF_HEADS, 2, DIFF_DH)
        return q, k, z['diff_v'].reshape(B, T, DIFF_HEADS, HEAD_DIM)

    qc, kc, vc = prep(zc)
    ql, kl, vl = prep(zl)
    ql = axial_rope(ql, pos_r, pos_c)
    kl = axial_rope(kl, pos_r, pos_c)
    lq1, lk1, lq2, lk2 = lp['diff_lambda'].astype(jnp.float32)
    lam = jnp.exp(jnp.sum(lq1 * lk1)) - jnp.exp(jnp.sum(lq2 * lk2)) + lam_init
    k_all = jnp.concatenate([kc, kl], axis=1)
    v_all = jnp.concatenate([vc, vl], axis=1)
    attend = lambda qi, kk, vv: diff_attend(qi, kk, vv, lam)

    def post(o):
        B, T = o.shape[:2]
        return (rms_norm(o, lp['diff_norm_w']) * (1.0 - lam_init)).reshape(B, T, GROUP_WIDTH)

    o_lat = post(blocked_queries(attend, ql, k_all, v_all))
    o_ctx = post(attend(qc, kc, vc)) if need_ctx else None
    return o_ctx, o_lat


def token_mixers(h_ctx, h_lat, lp, lam_init, need_ctx):
    zc = split_columns(h_ctx @ lp['w_in'])
    zl = split_columns(h_lat @ lp['w_in'])
    n_lat = h_lat.shape[1]
    rows = n_lat // GRID_W
    pos_r = jnp.repeat(jnp.arange(rows), GRID_W)
    pos_c = jnp.tile(jnp.arange(GRID_W), rows)
    parts = [gla_mixer(zc, zl, lp, need_ctx),
             deltanet_mixer(zc, zl, lp, need_ctx),
             gqa_mixer(zc, zl, lp, pos_r, pos_c, need_ctx),
             diff_mixer(zc, zl, lp, lam_init, pos_r, pos_c, need_ctx)]
    o_lat = jnp.concatenate([p[1] for p in parts], axis=-1) @ lp['w_out']
    o_ctx = jnp.concatenate([p[0] for p in parts], axis=-1) @ lp['w_out'] if need_ctx else None
    return o_ctx, o_lat


def expert_choice_moe(h, router_w, w_gate, w_up, w_down):
    B, N, D = h.shape
    cap = EC_CAPACITY * N // N_EXPERTS
    aff = jax.nn.softmax((h @ router_w).astype(jnp.float32), axis=-1)
    weight, idx = lax.top_k(jnp.swapaxes(aff, 1, 2), cap)
    xs = jax.vmap(lambda hb, ib: hb[ib])(h, idx)
    hid = jax.nn.silu(jnp.einsum('becd,edf->becf', xs, w_gate)) * jnp.einsum('becd,edf->becf', xs, w_up)
    y = jnp.einsum('becf,efd->becd', hid, w_down) * weight[..., None].astype(h.dtype)
    return jax.vmap(lambda yb, ib: jnp.zeros((N, D), h.dtype).at[ib].add(yb))(y, idx)


def setup_inputs(seed: int = 0) -> dict:
    key = jax.random.key(seed)
    ks = jax.random.split(key, 26)
    f32 = jnp.float32
    nrm = lambda k, shape, scale: jax.random.normal(k, shape, f32) * scale
    D = D_MODEL
    L = DEPTH
    d_in = sum(s for _, s in IN_SPLITS)
    dt = jnp.exp(jax.random.uniform(ks[15], (L, 2, DN_HEADS), f32, math.log(1e-3), math.log(1e-1)))
    return {
        'x': nrm(ks[0], (BATCH, SEQ, D), 1.0),
        'c': nrm(ks[1], (BATCH, D), 1.0),
        'ctx': nrm(ks[2], (BATCH, CTX_LEN, D), 1.0),
        'c_ctx': nrm(ks[3], (D,), 1.0),
        'mod_w': nrm(ks[4], (L, D, 6 * D), 0.5 * D ** -0.5),
        'mod_b': nrm(ks[5], (L, 6 * D), 0.02),
        'norm1_w': 1.0 + nrm(ks[6], (L, D), 0.05),
        'norm2_w': 1.0 + nrm(ks[7], (L, D), 0.05),
        'w_in': nrm(ks[8], (L, D, d_in), D ** -0.5),
        'w_out': nrm(ks[9], (L, MIX_WIDTH, D), MIX_WIDTH ** -0.5),
        'gla_gate_w2': nrm(ks[10], (L, 2, GLA_GATE_RANK, GLA_HEADS * GLA_DK), GLA_GATE_RANK ** -0.5),
        'gla_gate_b': nrm(ks[11], (L, 2, GLA_HEADS * GLA_DK), 0.1),
        'gla_norm_w': 1.0 + nrm(ks[12], (L, HEAD_DIM), 0.05),
        'dn_conv_w': nrm(ks[13], (L, SHORT_CONV, 3 * DN_HEADS * HEAD_DIM), SHORT_CONV ** -0.5),
        'dn_a_log': jnp.log(jax.random.uniform(ks[14], (L, 2, DN_HEADS), f32, 1.0, 16.0)),
        'dn_dt_bias': dt + jnp.log(-jnp.expm1(-dt)),
        'dn_norm_w': 1.0 + nrm(ks[16], (L, HEAD_DIM), 0.05),
        'gqa_q_norm': 1.0 + nrm(ks[17], (L, HEAD_DIM), 0.05),
        'gqa_k_norm': 1.0 + nrm(ks[18], (L, HEAD_DIM), 0.05),
        'diff_lambda': nrm(ks[19], (L, 4, DIFF_DH), 0.1),
        'diff_norm_w': 1.0 + nrm(ks[20], (L, HEAD_DIM), 0.05),
        'router_w': nrm(ks[21], (L, D, N_EXPERTS), D ** -0.5),
        'exp_w_gate': nrm(ks[22], (L, N_EXPERTS, D, EXPERT_FF), D ** -0.5),
        'exp_w_up': nrm(ks[23], (L, N_EXPERTS, D, EXPERT_FF), D ** -0.5),
        'exp_w_down': nrm(ks[24], (L, N_EXPERTS, EXPERT_FF, D), EXPERT_FF ** -0.5),
        'final_norm_w': 1.0 + nrm(ks[25], (D,), 0.05),
    }


def reference(x, c, ctx, c_ctx, mod_w, mod_b, norm1_w, norm2_w, w_in, w_out,
              gla_gate_w2, gla_gate_b, gla_norm_w, dn_conv_w, dn_a_log, dn_dt_bias, dn_norm_w,
              gqa_q_norm, gqa_k_norm, diff_lambda, diff_norm_w,
              router_w, exp_w_gate, exp_w_up, exp_w_down, final_norm_w):
    ctx_h = ctx
    for l in range(DEPTH):
        last = l == DEPTH - 1
        lp = {'w_in': w_in[l], 'w_out': w_out[l],
              'gla_gate_w2': gla_gate_w2[l], 'gla_gate_b': gla_gate_b[l], 'gla_norm_w': gla_norm_w[l],
              'dn_conv_w': dn_conv_w[l], 'dn_a_log': dn_a_log[l], 'dn_dt_bias': dn_dt_bias[l],
              'dn_norm_w': dn_norm_w[l], 'gqa_q_norm': gqa_q_norm[l], 'gqa_k_norm': gqa_k_norm[l],
              'diff_lambda': diff_lambda[l], 'diff_norm_w': diff_norm_w[l]}
        lam_init = 0.8 - 0.6 * math.exp(-0.3 * l)
        sh1, sc1, g1, sh2, sc2, g2 = [m[:, None, :] for m in
                                      jnp.split(jax.nn.silu(c) @ mod_w[l] + mod_b[l], 6, axis=-1)]
        csh1, csc1, cg1, csh2, csc2, cg2 = jnp.split(jax.nn.silu(c_ctx) @ mod_w[l] + mod_b[l], 6, axis=-1)
        h_lat = rms_norm(x, norm1_w[l]) * (1.0 + sc1) + sh1
        h_ctx = rms_norm(ctx_h, norm1_w[l]) * (1.0 + csc1) + csh1
        o_ctx, o_lat = token_mixers(h_ctx, h_lat, lp, lam_init, not last)
        x = x + g1 * o_lat
        h2 = rms_norm(x, norm2_w[l]) * (1.0 + sc2) + sh2
        x = x + g2 * expert_choice_moe(h2, router_w[l], exp_w_gate[l], exp_w_up[l], exp_w_down[l])
        if not last:
            ctx_h = ctx_h + cg1 * o_ctx
            hc2 = rms_norm(ctx_h, norm2_w[l]) * (1.0 + csc2) + csh2
            ctx_h = ctx_h + cg2 * expert_choice_moe(hc2, router_w[l], exp_w_gate[l], exp_w_up[l], exp_w_down[l])
    return rms_norm(x, final_norm_w)
```

```python
import functools
import math

import jax
import jax.numpy as jnp
import numpy as np
from jax import lax
from jax.experimental import pallas as pl
from jax.experimental.pallas import tpu as pltpu

F32 = jnp.float32
_MXU = jnp.bfloat16
_ACT = jnp.bfloat16

D_MODEL = 2048
N_CTX = 256
GRID_W = 64
HEAD_DIM = 128
N_HEADS = 4
GROUP_WIDTH = N_HEADS * HEAD_DIM
GLA_DK = 64
GLA_RANK = 16
GLA_TAU = 16.0
SHORT_CONV = 5
GQA_KV_HEADS = 2
DIFF_DH = 64
ROPE_THETA = 10000.0
N_EXPERTS = 16
EC_CAPACITY = 2
EXPERT_FF = D_MODEL // 2
RMS_EPS = 1e-6

LANES = 128
ROW_TILE = 256
CHUNK = 64
VMEM_LIMIT = 56 * 1024 * 1024

Z_GLA_QK = 0
Z_GLA_V = 512
Z_GLA_G = 1024
Z_DN_QKV = 1536
Z_DN_G = 3072
Z_GQA_Q = 3584
Z_GQA_KV = 4096
Z_DIFF_Q = 4608
Z_DIFF_K = 5120
Z_DIFF_V = 5632
Z_WIDTH = 6144
M_LR = 0
M_DNA = 32
M_DNB = 40


def _cparams(sem, vmem=VMEM_LIMIT):
    return pltpu.CompilerParams(dimension_semantics=sem, vmem_limit_bytes=vmem)


def _dot(a, b):
    return jnp.dot(a.astype(_MXU), b.astype(_MXU), preferred_element_type=F32)


def _dot_nt(a, b):
    return lax.dot_general(a.astype(_MXU), b.astype(_MXU), (((1,), (1,)), ((), ())),
                           preferred_element_type=F32)


def _dot_tn(a, b):
    return lax.dot_general(a.astype(_MXU), b.astype(_MXU), (((0,), (0,)), ((), ())),
                           preferred_element_type=F32)


def _split(a, n):
    parts = []
    r = a
    for _ in range(n - 1):
        p = r.astype(_MXU)
        parts.append(p)
        r = r - p.astype(F32)
    parts.append(r.astype(_MXU))
    return parts


def _dot_hi(a, b):
    a0, a1 = _split(a, 2)
    b0, b1 = _split(b, 2)
    return _dot(a0, b0) + (_dot(a1, b0) + _dot(a0, b1))


def _dot_exact_l(e, a, fn=_dot):
    a0, a1, a2 = _split(a, 3)
    return fn(e, a0) + (fn(e, a1) + fn(e, a2))


def _dot_exact_r(a, e, fn=_dot):
    a0, a1, a2 = _split(a, 3)
    return fn(a0, e) + (fn(a1, e) + fn(a2, e))


def _silu(x):
    return x * jax.nn.sigmoid(x)


def _iota(shape, dim):
    return lax.broadcasted_iota(jnp.int32, shape, dim)


def _mod_kernel(cc_ref, w_ref, b_ref, o_ref):
    a = _silu(cc_ref[...])
    o_ref[0] = _dot_hi(a, w_ref[0]) + b_ref[0]


def _modulation(cc, mod_w, mod_b):
    L, D, W = mod_w.shape
    tn = 1024
    return pl.pallas_call(
        _mod_kernel,
        grid=(L, W // tn),
        in_specs=[pl.BlockSpec((8, D), lambda l, j: (0, 0)),
                  pl.BlockSpec((1, D, tn), lambda l, j: (l, 0, j)),
                  pl.BlockSpec((1, 1, tn), lambda l, j: (l, 0, j))],
        out_specs=pl.BlockSpec((1, 8, tn), lambda l, j: (l, 0, j)),
        out_shape=jax.ShapeDtypeStruct((L, 8, W), F32),
        compiler_params=_cparams(("parallel", "parallel")),
        name="modulation",
    )(cc, mod_w, mod_b.reshape(L, 1, W))


def _in_proj_kernel(x_ref, mt_ref, nw_ref, wb_ref, wm_ref, z_ref, zm_ref):
    x = x_ref[0]
    y = x * lax.rsqrt(jnp.mean(x * x, axis=-1, keepdims=True) + RMS_EPS) * nw_ref[...]
    h = y * (1.0 + mt_ref[0, 0, 1:2, :]) + mt_ref[0, 0, 0:1, :]
    hb = h.astype(_MXU)
    z_ref[0] = jnp.dot(hb, wb_ref[...], preferred_element_type=F32).astype(z_ref.dtype)

    @pl.when(pl.program_id(0) == 0)
    def _():
        zm_ref[0, 0] = jnp.dot(hb, wm_ref[...], preferred_element_type=F32)

    @pl.when(pl.program_id(0) != 0)
    def _():
        zm_ref[0, 0] = jnp.zeros(zm_ref.shape[2:], F32)


def _in_proj(tok, mtab, norm_w, w_big, w_misc, n_lat_tiles):
    B, T, D = tok.shape
    nt = T // ROW_TILE
    nj = 2
    tn = Z_WIDTH // nj
    z, zm = pl.pallas_call(
        _in_proj_kernel,
        grid=(nj, B, nt),
        in_specs=[pl.BlockSpec((1, ROW_TILE, D), lambda j, b, m: (b, m, 0)),
                  pl.BlockSpec((1, 1, 8, D), lambda j, b, m: (b, m // n_lat_tiles, 0, 0)),
                  pl.BlockSpec((1, D), lambda j, b, m: (0, 0)),
                  pl.BlockSpec((D, tn), lambda j, b, m: (0, j)),
                  pl.BlockSpec((D, LANES), lambda j, b, m: (0, 0))],
        out_specs=[pl.BlockSpec((1, ROW_TILE, tn), lambda j, b, m: (b, m, j)),
                   pl.BlockSpec((1, 1, ROW_TILE, LANES), lambda j, b, m: (j, b, m, 0))],
        out_shape=[jax.ShapeDtypeStruct((B, T, Z_WIDTH), _ACT),
                   jax.ShapeDtypeStruct((nj, B, T, LANES), F32)],
        compiler_params=_cparams(("arbitrary", "arbitrary", "arbitrary")),
        name="in_proj",
    )(tok, mtab, norm_w.reshape(1, D), w_big, w_misc)
    return z, zm[0]


def _rope(y, cos, sin, w):
    n = y.shape[-1]
    lane = _iota(y.shape, 1)
    partner = jnp.where((lane % (2 * w)) < w, pltpu.roll(y, n - w, 1), pltpu.roll(y, w, 1))
    return y * cos + partner * sin


def _attn_prep_kernel(gq_ref, gk_ref, dq_ref, dk_ref, qn_ref, kn_ref, cg_ref, sg_ref, cd_ref, sd_ref,
                      ogq_ref, ogk_ref, odq_ref, odk_ref):
    cg, sg, cd, sd = cg_ref[...], sg_ref[...], cd_ref[...], sd_ref[...]

    def head_norm(x, w):
        return x * lax.rsqrt(jnp.mean(x * x, axis=-1, keepdims=True) + RMS_EPS) * w

    for h in range(N_HEADS):
        sl = slice(h * HEAD_DIM, (h + 1) * HEAD_DIM)
        q = head_norm(gq_ref[0, :, sl].astype(F32), qn_ref[...])
        ogq_ref[0, :, sl] = (_rope(q, cg, sg, 32) * HEAD_DIM ** -0.5).astype(ogq_ref.dtype)
        dq = _rope(dq_ref[0, :, sl].astype(F32), cd, sd, 16) * DIFF_DH ** -0.5
        odq_ref[0, :, sl] = dq.astype(odq_ref.dtype)
        odk_ref[0, :, sl] = _rope(dk_ref[0, :, sl].astype(F32), cd, sd, 16).astype(odk_ref.dtype)
    for h in range(GQA_KV_HEADS):
        sl = slice(h * HEAD_DIM, (h + 1) * HEAD_DIM)
        k = head_norm(gk_ref[0, :, sl].astype(F32), kn_ref[...])
        ogk_ref[0, :, sl] = _rope(k, cg, sg, 32).astype(ogk_ref.dtype)


def _attn_prep(z, q_norm, k_norm, tabs):
    B, T, _ = z.shape
    nt = T // ROW_TILE
    zspec = lambda w, off: pl.BlockSpec((1, ROW_TILE, w), lambda b, m: (b, m, off // w))
    tspec = pl.BlockSpec((ROW_TILE, LANES), lambda b, m: (m, 0))
    wspec = pl.BlockSpec((1, LANES), lambda b, m: (0, 0))
    ospec = lambda w: pl.BlockSpec((1, ROW_TILE, w), lambda b, m: (b, m, 0))
    return pl.pallas_call(
        _attn_prep_kernel,
        grid=(B, nt),
        in_specs=[zspec(512, Z_GQA_Q), zspec(256, Z_GQA_KV), zspec(512, Z_DIFF_Q), zspec(512, Z_DIFF_K),
                  wspec, wspec, tspec, tspec, tspec, tspec],
        out_specs=[ospec(512), ospec(256), ospec(512), ospec(512)],
        out_shape=[jax.ShapeDtypeStruct((B, T, 512), _ACT), jax.ShapeDtypeStruct((B, T, 256), _ACT),
                   jax.ShapeDtypeStruct((B, T, 512), _ACT), jax.ShapeDtypeStruct((B, T, 512), _ACT)],
        compiler_params=_cparams(("parallel", "parallel")),
        name="attn_prep",
    )(z, z, z, z, q_norm.reshape(1, LANES), k_norm.reshape(1, LANES), *tabs)


def _rope_tables(n_lat):
    t = np.arange(n_lat)
    pos_r, pos_c = t // GRID_W, t % GRID_W

    def tab(d, reps):
        half = d // 2
        inv = ROPE_THETA ** (-np.arange(0, half, 2, dtype=np.float32) / half)
        ar = pos_r[:, None].astype(np.float32) * inv
        ac = pos_c[:, None].astype(np.float32) * inv
        ar, ac = jnp.asarray(ar, F32), jnp.asarray(ac, F32)
        cos = jnp.concatenate([jnp.cos(ar), jnp.cos(ar), jnp.cos(ac), jnp.cos(ac)], axis=1)
        sin = jnp.concatenate([-jnp.sin(ar), jnp.sin(ar), -jnp.sin(ac), jnp.sin(ac)], axis=1)
        cos, sin = jnp.tile(cos, (1, reps)), jnp.tile(sin, (1, reps))
        cos = jnp.concatenate([cos, jnp.ones((N_CTX, LANES), F32)], axis=0)
        sin = jnp.concatenate([sin, jnp.zeros((N_CTX, LANES), F32)], axis=0)
        return cos, sin

    cg, sg = tab(HEAD_DIM, 1)
    cd, sd = tab(DIFF_DH, 2)
    return cg, sg, cd, sd


def _softmax_step(s, m_ref, l_ref, rows):
    m_old = m_ref[rows, :]
    m_new = jnp.maximum(m_old, jnp.max(s, axis=-1, keepdims=True))
    alpha = jnp.exp(m_old - m_new)
    p = jnp.exp(s - m_new)
    l_ref[rows, :] = alpha * l_ref[rows, :] + jnp.sum(p, axis=-1, keepdims=True)
    m_ref[rows, :] = m_new
    return p, alpha


def _gqa_kernel(q_ref, k_ref, v_ref, o_ref, acc_ref, m_ref, l_ref, *, tq, tk, k_lo, k_hi):
    rep = N_HEADS // GQA_KV_HEADS
    q2 = jnp.concatenate([q_ref[0, :, r * HEAD_DIM:(r + 1) * HEAD_DIM] for r in range(rep)], axis=0)
    acc_ref[...] = jnp.zeros_like(acc_ref)
    m_ref[...] = jnp.full_like(m_ref, -jnp.inf)
    l_ref[...] = jnp.zeros_like(l_ref)
    allrows = slice(0, rep * tq)

    def body(j, _):
        ks = pl.multiple_of(j * tk, tk)
        kc = k_ref[0, pl.ds(ks, tk), :]
        vc = v_ref[0, pl.ds(ks, tk), :]
        s = _dot_nt(q2, kc)
        p, alpha = _softmax_step(s, m_ref, l_ref, allrows)
        acc_ref[...] = alpha * acc_ref[...] + _dot(p, vc)
        return 0

    lax.fori_loop(k_lo, k_hi, body, 0)
    o = acc_ref[...] / l_ref[...]
    for r in range(rep):
        o_ref[0, :, r * HEAD_DIM:(r + 1) * HEAD_DIM] = o[r * tq:(r + 1) * tq].astype(o_ref.dtype)


def _gqa_attention(gq, gk, z, prev, *, tq, q_lo, nq, k_lo, k_hi):
    B, T, _ = gq.shape
    tk = ROW_TILE
    rep = N_HEADS // GQA_KV_HEADS
    kern = functools.partial(_gqa_kernel, tq=tq, tk=tk, k_lo=k_lo, k_hi=k_hi)
    in_specs = [pl.BlockSpec((1, tq, rep * HEAD_DIM), lambda b, g, i: (b, q_lo + i, g)),
                pl.BlockSpec((1, T, HEAD_DIM), lambda b, g, i: (b, 0, g)),
                pl.BlockSpec((1, T, HEAD_DIM), lambda b, g, i: (b, 0, Z_GQA_KV // HEAD_DIM + GQA_KV_HEADS + g))]
    args = [gq, gk, z]
    aliases = {}
    if prev is not None:
        in_specs.append(pl.BlockSpec(memory_space=pl.ANY))
        args.append(prev)
        aliases = {3: 0}
        kern = functools.partial(_drop_arg, kern, 3)
    return pl.pallas_call(
        kern,
        grid=(B, GQA_KV_HEADS, nq),
        in_specs=in_specs,
        out_specs=pl.BlockSpec((1, tq, rep * HEAD_DIM), lambda b, g, i: (b, q_lo + i, g)),
        out_shape=jax.ShapeDtypeStruct((B, T, GROUP_WIDTH), _ACT),
        scratch_shapes=[pltpu.VMEM((rep * tq, HEAD_DIM), F32), pltpu.VMEM((rep * tq, 1), F32),
                        pltpu.VMEM((rep * tq, 1), F32)],
        input_output_aliases=aliases,
        compiler_params=_cparams(("parallel", "parallel", "parallel")),
        name="gqa_attention",
    )(*args)


def _drop_arg(fn, idx, *refs):
    return fn(*(refs[:idx] + refs[idx + 1:]))


def _diff_kernel(q_ref, k_ref, v_ref, lam_ref, nw_ref, o_ref, acc_ref, m_ref, l_ref,
                 *, tq, tk, k_lo, k_hi, lam_init):
    q = q_ref[0]
    lane = _iota(q.shape, 1)
    zero = jnp.zeros_like(q)
    q1 = jnp.where(lane < DIFF_DH, q, zero)
    q2 = jnp.where(lane >= DIFF_DH, q, zero)
    acc_ref[...] = jnp.zeros_like(acc_ref)
    m_ref[...] = jnp.full_like(m_ref, -jnp.inf)
    l_ref[...] = jnp.zeros_like(l_ref)
    r1, r2 = slice(0, tq), slice(tq, 2 * tq)

    def body(j, _):
        ks = pl.multiple_of(j * tk, tk)
        kc = k_ref[0, pl.ds(ks, tk), :]
        vc = v_ref[0, pl.ds(ks, tk), :]
        p1, a1 = _softmax_step(_dot_nt(q1, kc), m_ref, l_ref, r1)
        p2, a2 = _softmax_step(_dot_nt(q2, kc), m_ref, l_ref, r2)
        pv = _dot(jnp.concatenate([p1, p2], axis=0), vc)
        acc_ref[r1, :] = a1 * acc_ref[r1, :] + pv[:tq]
        acc_ref[r2, :] = a2 * acc_ref[r2, :] + pv[tq:]
        return 0

    lax.fori_loop(k_lo, k_hi, body, 0)
    lm = lam_ref[...]
    lam = (jnp.exp(jnp.sum(lm[0:1] * lm[1:2], axis=-1, keepdims=True))
           - jnp.exp(jnp.sum(lm[2:3] * lm[3:4], axis=-1, keepdims=True)) + lam_init)
    o = acc_ref[r1, :] / l_ref[r1, :] - lam * (acc_ref[r2, :] / l_ref[r2, :])
    y = o * lax.rsqrt(jnp.mean(o * o, axis=-1, keepdims=True) + RMS_EPS) * nw_ref[...]
    o_ref[0] = (y * (1.0 - lam_init)).astype(o_ref.dtype)


def _diff_attention(dq, dk, z, lam_p, norm_w, prev, *, lam_init, tq, q_lo, nq, k_lo, k_hi):
    B, T, _ = dq.shape
    tk = ROW_TILE
    kern = functools.partial(_diff_kernel, tq=tq, tk=tk, k_lo=k_lo, k_hi=k_hi, lam_init=lam_init)
    in_specs = [pl.BlockSpec((1, tq, HEAD_DIM), lambda b, h, i: (b, q_lo + i, h)),
                pl.BlockSpec((1, T, HEAD_DIM), lambda b, h, i: (b, 0, h)),
                pl.BlockSpec((1, T, HEAD_DIM), lambda b, h, i: (b, 0, Z_DIFF_V // HEAD_DIM + h)),
                pl.BlockSpec((8, LANES), lambda b, h, i: (0, 0)),
                pl.BlockSpec((1, LANES), lambda b, h, i: (0, 0))]
    args = [dq, dk, z, lam_p, norm_w.reshape(1, LANES)]
    aliases = {}
    if prev is not None:
        in_specs.append(pl.BlockSpec(memory_space=pl.ANY))
        args.append(prev)
        aliases = {5: 0}
        kern = functools.partial(_drop_arg, kern, 5)
    return pl.pallas_call(
        kern,
        grid=(B, N_HEADS, nq),
        in_specs=in_specs,
        out_specs=pl.BlockSpec((1, tq, HEAD_DIM), lambda b, h, i: (b, q_lo + i, h)),
        out_shape=jax.ShapeDtypeStruct((B, T, GROUP_WIDTH), _ACT),
        scratch_shapes=[pltpu.VMEM((2 * tq, HEAD_DIM), F32), pltpu.VMEM((2 * tq, 1), F32),
                        pltpu.VMEM((2 * tq, 1), F32)],
        input_output_aliases=aliases,
        compiler_params=_cparams(("parallel", "parallel", "parallel")),
        name="diff_attention",
    )(*args)


def _log_sigmoid(x):
    return jnp.minimum(x, 0.0) - jnp.log(1.0 + jnp.exp(-jnp.abs(x)))


def _softplus(x):
    return jnp.maximum(x, 0.0) + jnp.log(1.0 + jnp.exp(-jnp.abs(x)))


def _tri(n, kind):
    ii, jj = _iota((n, n), 0), _iota((n, n), 1)
    return {"le": ii >= jj, "lt": ii > jj, "ue": ii <= jj, "ut": ii < jj}[kind]


def _chain_orders(n_lat, n_ctx):
    fwd = [(n_lat, n_ctx, 1), (0, n_lat, 1)]
    rev = [(n_lat + n_ctx - 1, n_ctx, -1), (n_lat - 1, n_lat, -1)]
    return fwd, rev


def _head_post(o, nw, g):
    y = o * lax.rsqrt(jnp.mean(o * o, axis=-1, keepdims=True) + RMS_EPS) * nw
    return y * _silu(g)


def _gla_kernel(qk_ref, v_ref, g_ref, zm_ref, gw_ref, gb_ref, nw_ref, o_ref,
                u_ref, dl_ref, oi_ref, qg_ref, *, n_lat, n_ctx):
    C = CHUNK
    nch = n_lat + n_ctx
    le, ue = _tri(C, "le"), _tri(C, "ue")
    ltri, utri = le.astype(F32), ue.astype(F32)
    first = _iota((C, LANES), 1) < GLA_DK
    ones = jnp.ones((C, LANES), F32)
    gw, gb = gw_ref[0], gb_ref[0]
    qscale = GLA_DK ** -0.5

    def phase1(n, _):
        rows = pl.ds(pl.multiple_of(n * C, C), C)
        qk = qk_ref[0, rows, :].astype(F32)
        v = v_ref[0, rows, :]
        la = _log_sigmoid(_dot_hi(zm_ref[0, rows, :], gw) + gb) * (1.0 / GLA_TAU)
        b = jnp.where(first, _dot_exact_l(ltri, la), _dot_exact_l(utri, la))
        btot = jnp.where(first[0:1], b[C - 1:C, :], b[0:1, :])
        ref = b[C // 2:C // 2 + 1, :]
        qksw = pltpu.roll(qk, GLA_DK, 1)
        e1, e2 = jnp.exp(b - ref), jnp.exp(ref - b)
        zero = jnp.zeros_like(qk)
        qf = jnp.where(first, qk * e1, zero) * qscale
        kf = jnp.where(first, qksw * e2, zero)
        qr = jnp.where(first, zero, qksw * e1) * qscale
        kr = jnp.where(first, zero, qk * e2)
        a = jnp.where(le, _dot_nt(qf, kf), 0.0) + jnp.where(ue, _dot_nt(qr, kr), 0.0)
        oi_ref[rows, :] = _dot(a, v)
        qg_ref[rows, :] = jnp.where(first, qk, qksw) * jnp.exp(b) * qscale
        kd = jnp.where(first, qksw, qk) * jnp.exp(btot - b)
        u_ref[n] = _dot_tn(kd, v)
        dl_ref[n] = jnp.exp(_dot_exact_r(la, ones, _dot_tn))
        return 0

    lax.fori_loop(0, nch, phase1, 0)

    fwd, rev = _chain_orders(n_lat, n_ctx)
    hf, hr = slice(0, GLA_DK), slice(GLA_DK, 2 * GLA_DK)
    state = (jnp.zeros((GLA_DK, LANES), F32), jnp.zeros((GLA_DK, LANES), F32))
    for (f0, cnt, _), (r0, _, _) in zip(fwd, rev):
        def step(t, st, f0=f0, r0=r0):
            sf, sr = st
            cf, cr = f0 + t, r0 - t
            uf, df = u_ref[cf, hf, :], dl_ref[cf, hf, :]
            ur, dr = u_ref[cr, hr, :], dl_ref[cr, hr, :]
            u_ref[cf, hf, :] = sf
            u_ref[cr, hr, :] = sr
            return df * sf + uf, dr * sr + ur
        state = lax.fori_loop(0, cnt, step, state)

    def phase3(n, _):
        rows = pl.ds(pl.multiple_of(n * C, C), C)
        o = oi_ref[rows, :] + _dot(qg_ref[rows, :], u_ref[n])
        o_ref[0, rows, :] = _head_post(o, nw_ref[...], g_ref[0, rows, :].astype(F32)).astype(o_ref.dtype)
        return 0

    lax.fori_loop(0, nch, phase3, 0)


def _gla_mixer(z, zm, gw, gb, norm_w, n_lat_rows):
    B, T, _ = z.shape
    n_lat, n_ctx = n_lat_rows // CHUNK, (T - n_lat_rows) // CHUNK
    nch = n_lat + n_ctx
    zspec = lambda off: pl.BlockSpec((1, T, LANES), lambda b, h: (b, 0, off // LANES + h))
    return pl.pallas_call(
        functools.partial(_gla_kernel, n_lat=n_lat, n_ctx=n_ctx),
        grid=(B, N_HEADS),
        in_specs=[zspec(Z_GLA_QK), zspec(Z_GLA_V), zspec(Z_GLA_G),
                  pl.BlockSpec((1, T, LANES), lambda b, h: (b, 0, 0)),
                  pl.BlockSpec((1, LANES, LANES), lambda b, h: (h, 0, 0)),
                  pl.BlockSpec((1, 1, LANES), lambda b, h: (h, 0, 0)),
                  pl.BlockSpec((1, LANES), lambda b, h: (0, 0))],
        out_specs=pl.BlockSpec((1, T, LANES), lambda b, h: (b, 0, h)),
        out_shape=jax.ShapeDtypeStruct((B, T, GROUP_WIDTH), _ACT),
        scratch_shapes=[pltpu.VMEM((nch, LANES, LANES), F32), pltpu.VMEM((nch, LANES, LANES), F32),
                        pltpu.VMEM((T, LANES), F32), pltpu.VMEM((T, LANES), F32)],
        compiler_params=_cparams(("parallel", "parallel")),
        name="gla_mixer",
    )(z, z, z, zm, gw, gb, norm_w.reshape(1, LANES))


def _dn_gates(zm, gp, h):
    C = zm.shape[0]
    lane = _iota(zm.shape, 1)
    ltri, utri = _tri(C, "le").astype(F32), _tri(C, "ue").astype(F32)
    ri = _iota((8, LANES), 0)
    li = _iota((8, LANES), 1)
    sel_rows = ((li == M_DNA + h) & (ri == 0)) | ((li == M_DNA + N_HEADS + h) & (ri == 1))
    a_rows = _dot_exact_r(zm, sel_rows.astype(F32), lambda a, e: _dot_nt(e, a))
    out = []
    for d in range(2):
        pick = lambda off: jnp.sum(jnp.where(lane == off + d * N_HEADS + h, zm, 0.0), axis=-1, keepdims=True)
        na = -jnp.exp(gp[d:d + 1, :])
        dt = gp[2 + d:3 + d, :]
        g_col = na * _softplus(pick(M_DNA) + dt)
        g_row = na[:, :C] * _softplus(a_rows[d:d + 1, :] + dt[:, :C])
        tri = ltri if d == 0 else utri
        gc_col = _dot_exact_l(tri, g_col)
        gc_row = _dot_exact_r(jnp.broadcast_to(g_row, (8, C)), tri, _dot_nt)[0:1, :]
        beta = jax.nn.sigmoid(pick(M_DNB))
        out.append((beta, gc_col, gc_row))
    return out


def _dn_decay(gc_col, gc_row, mask):
    C = gc_row.shape[1]
    diff = jnp.where(mask, gc_col[:, :C] - gc_row, 0.0)
    return jnp.where(mask, jnp.exp(diff), 0.0)


def _dn_prep_kernel(qkv_ref, zm_ref, cw_ref, gp_ref, qkvn_ref, l_ref, xp_ref, *, n_lat_rows):
    C = CHUNK
    T = qkv_ref.shape[1]
    nch = T // C
    PAD = 8
    W = 3 * HEAD_DIM
    h = pl.program_id(1)
    xp_ref[0:PAD, :] = jnp.zeros((PAD, W), F32)
    xp_ref[PAD + T:PAD + T + PAD, :] = jnp.zeros((PAD, W), F32)

    def fill(n, _):
        rows = pl.ds(pl.multiple_of(n * C, C), C)
        xp_ref[pl.ds(pl.multiple_of(PAD + n * C, 8), C), :] = qkv_ref[0, rows, :].astype(F32)
        return 0

    lax.fori_loop(0, nch, fill, 0)
    cw = cw_ref[0]
    gp = gp_ref[0]
    lt, ut = _tri(C, "lt"), _tri(C, "ut")
    tcol = _iota((C, 1), 0)
    half = SHORT_CONV // 2

    def chunk(n, _):
        r0 = pl.multiple_of(n * C, C)
        rows = pl.ds(r0, C)
        win = xp_ref[pl.ds(r0, C + 2 * PAD), :]
        seg = r0 >= n_lat_rows
        acc = jnp.zeros((C, W), F32)
        for i in range(SHORT_CONV):
            xs = win[PAD + i - half:PAD + i - half + C, :]
            if i != half:
                same = ((r0 + tcol + (i - half)) >= n_lat_rows) == seg
                xs = jnp.where(same, xs, 0.0)
            acc = acc + cw[i:i + 1, :] * xs
        y = _silu(acc)
        q, k, v = y[:, :HEAD_DIM], y[:, HEAD_DIM:2 * HEAD_DIM], y[:, 2 * HEAD_DIM:]
        qn = q * lax.rsqrt(jnp.sum(q * q, axis=-1, keepdims=True) + RMS_EPS) * HEAD_DIM ** -0.5
        kn = k * lax.rsqrt(jnp.sum(k * k, axis=-1, keepdims=True) + RMS_EPS)
        qkvn_ref[0, 0, rows, :] = jnp.concatenate([qn, kn, v], axis=1).astype(qkvn_ref.dtype)
        kk = _dot_nt(kn, kn)
        (bf, gcf, grf), (br, gcr, grr) = _dn_gates(zm_ref[0, rows, :], gp, h)
        l_ref[0, 0, 0, n] = kk * bf * _dn_decay(gcf, grf, lt)
        l_ref[0, 0, 1, n] = kk * br * _dn_decay(gcr, grr, ut)
        return 0

    lax.fori_loop(0, nch, chunk, 0)


def _dn_prep(z, zm, conv_w, gp, n_lat_rows):
    B, T, _ = z.shape
    nch = T // CHUNK
    W = 3 * HEAD_DIM
    return pl.pallas_call(
        functools.partial(_dn_prep_kernel, n_lat_rows=n_lat_rows),
        grid=(B, N_HEADS),
        in_specs=[pl.BlockSpec((1, T, W), lambda b, h: (b, 0, Z_DN_QKV // W + h)),
                  pl.BlockSpec((1, T, LANES), lambda b, h: (b, 0, 0)),
                  pl.BlockSpec((1, 8, W), lambda b, h: (h, 0, 0)),
                  pl.BlockSpec((1, 8, LANES), lambda b, h: (h, 0, 0))],
        out_specs=[pl.BlockSpec((1, 1, T, W), lambda b, h: (b, h, 0, 0)),
                   pl.BlockSpec((1, 1, 2, nch, CHUNK, CHUNK), lambda b, h: (b, h, 0, 0, 0, 0))],
        out_shape=[jax.ShapeDtypeStruct((B, N_HEADS, T, W), _ACT),
                   jax.ShapeDtypeStruct((B, N_HEADS, 2, nch, CHUNK, CHUNK), F32)],
        scratch_shapes=[pltpu.VMEM((T + 16, W), F32)],
        compiler_params=_cparams(("parallel", "parallel")),
        name="dn_prep",
    )(z, zm, conv_w, gp)


def _tri_solve_kernel(l_ref, t_ref):
    C = CHUNK
    t_ref[...] = jnp.zeros_like(t_ref)
    cidx = _iota((C, LANES), 0)

    def row(i, _):
        def blk(jb, acc):
            j0 = pl.multiple_of(jb * 8, 8)
            l8 = l_ref[0, i, pl.ds(j0, 8), :]
            for k in range(8):
                acc = acc - l8[k:k + 1, :] * t_ref[0, j0 + k]
            return acc
        acc = lax.fori_loop(0, (i + 7) // 8, blk, (cidx == i).astype(F32))
        t_ref[0, i] = acc
        return 0

    lax.fori_loop(0, C, row, 0)


def _tri_solve(lt):
    G = lt.shape[0]
    spec = pl.BlockSpec((1, CHUNK, CHUNK, LANES), lambda g: (g, 0, 0, 0))
    return pl.pallas_call(
        _tri_solve_kernel, grid=(G,), in_specs=[spec], out_specs=spec,
        out_shape=jax.ShapeDtypeStruct(lt.shape, F32),
        compiler_params=_cparams(("parallel",)),
        name="dn_tri_solve",
    )(lt)


def _dn_scan_kernel(qkvn_ref, t_ref, zm_ref, gp_ref, g_ref, nw_ref, o_ref,
                    mk_ref, ns_ref, dl_ref, u_ref, w_ref, qg_ref, at_ref, *, n_lat, n_ctx):
    C = CHUNK
    nch = n_lat + n_ctx
    h = pl.program_id(1)
    gp = gp_ref[0]
    masks = (_tri(C, "le"), _tri(C, "ue"))
    last_row = (C - 1, 0)

    def phase1(n, _):
        rows = pl.ds(pl.multiple_of(n * C, C), C)
        x = qkvn_ref[0, 0, rows, :].astype(F32)
        qn, kn, vs = x[:, :HEAD_DIM], x[:, HEAD_DIM:2 * HEAD_DIM], x[:, 2 * HEAD_DIM:]
        qk = _dot_nt(qn, kn)
        gates = _dn_gates(zm_ref[0, rows, :], gp, h)
        for d in range(2):
            beta, gc, gr = gates[d]
            egc = jnp.exp(gc)
            tm = t_ref[0, 0, d, n]
            u = _dot(tm, vs * beta)
            w = _dot(tm, kn * beta * egc)
            glast = gc[last_row[d]:last_row[d] + 1, :]
            kd = kn * jnp.exp(glast - gc)
            mk_ref[d, n] = (-_dot_tn(kd, w)).astype(mk_ref.dtype)
            ns_ref[d, n] = _dot_tn(kd, u)
            dl_ref[d, n] = jnp.broadcast_to(jnp.exp(glast), (8, LANES))
            u_ref[d, rows, :] = u
            w_ref[d, rows, :] = w.astype(w_ref.dtype)
            qg_ref[d, rows, :] = (qn * egc).astype(qg_ref.dtype)
            at_ref[d, n] = (qk * _dn_decay(gc, gr, masks[d])).astype(at_ref.dtype)
        return 0

    lax.fori_loop(0, nch, phase1, 0)

    fwd, rev = _chain_orders(n_lat, n_ctx)
    state = (jnp.zeros((HEAD_DIM, HEAD_DIM), F32), jnp.zeros((HEAD_DIM, HEAD_DIM), F32))
    for (f0, cnt, _), (r0, _, _) in zip(fwd, rev):
        def step(t, st, f0=f0, r0=r0):
            new = []
            for d, c in ((0, f0 + t), (1, r0 - t)):
                s = st[d]
                nn = ns_ref[d, c]
                ns_ref[d, c] = s
                new.append(dl_ref[d, c, 0:1, :] * s + _dot(mk_ref[d, c], s) + nn)
            return tuple(new)
        state = lax.fori_loop(0, cnt, step, state)

    def phase3(n, _):
        rows = pl.ds(pl.multiple_of(n * C, C), C)
        o = jnp.zeros((C, HEAD_DIM), F32)
        for d in range(2):
            s = ns_ref[d, n]
            ws = _dot(jnp.concatenate([w_ref[d, rows, :], qg_ref[d, rows, :]], axis=0), s)
            v_new = u_ref[d, rows, :] - ws[:C]
            o = o + ws[C:] + _dot(at_ref[d, n], v_new)
        o_ref[0, rows, :] = _head_post(o, nw_ref[...], g_ref[0, rows, :].astype(F32)).astype(o_ref.dtype)
        return 0

    lax.fori_loop(0, nch, phase3, 0)


def _dn_scan(qkvn, tmat, z, zm, gp, norm_w, n_lat_rows):
    B, T, _ = z.shape
    n_lat, n_ctx = n_lat_rows // CHUNK, (T - n_lat_rows) // CHUNK
    nch = n_lat + n_ctx
    W = 3 * HEAD_DIM
    return pl.pallas_call(
        functools.partial(_dn_scan_kernel, n_lat=n_lat, n_ctx=n_ctx),
        grid=(B, N_HEADS),
        in_specs=[pl.BlockSpec((1, 1, T, W), lambda b, h: (b, h, 0, 0)),
                  pl.BlockSpec((1, 1, 2, nch, CHUNK, CHUNK), lambda b, h: (b, h, 0, 0, 0, 0)),
                  pl.BlockSpec((1, T, LANES), lambda b, h: (b, 0, 0)),
                  pl.BlockSpec((1, 8, LANES), lambda b, h: (h, 0, 0)),
                  pl.BlockSpec((1, T, LANES), lambda b, h: (b, 0, Z_DN_G // LANES + h)),
                  pl.BlockSpec((1, LANES), lambda b, h: (0, 0))],
        out_specs=pl.BlockSpec((1, T, LANES), lambda b, h: (b, 0, h)),
        out_shape=jax.ShapeDtypeStruct((B, T, GROUP_WIDTH), _ACT),
        scratch_shapes=[pltpu.VMEM((2, nch, HEAD_DIM, HEAD_DIM), _MXU),
                        pltpu.VMEM((2, nch, HEAD_DIM, HEAD_DIM), F32),
                        pltpu.VMEM((2, nch, 8, LANES), F32),
                        pltpu.VMEM((2, T, HEAD_DIM), F32),
                        pltpu.VMEM((2, T, HEAD_DIM), _MXU),
                        pltpu.VMEM((2, T, HEAD_DIM), _MXU),
                        pltpu.VMEM((2, nch, CHUNK, CHUNK), _MXU)],
        compiler_params=_cparams(("parallel", "parallel")),
        name="dn_scan",
    )(qkvn, tmat, zm, gp, z, norm_w.reshape(1, LANES))


def _dn_mixer(z, zm, conv_w, gp, norm_w, n_lat_rows):
    B, T, _ = z.shape
    n_lat = n_lat_rows // CHUNK
    qkvn, lmat = _dn_prep(z, zm, conv_w, gp, n_lat_rows)
    parts = [lmat[:, :, 0, :n_lat], jnp.flip(lmat[:, :, 1, :n_lat], axis=(-2, -1)),
             lmat[:, :, 0, n_lat:], jnp.flip(lmat[:, :, 1, n_lat:], axis=(-2, -1))]
    sizes = [int(np.prod(p.shape[:-2])) for p in parts]
    flat = jnp.concatenate([p.reshape(-1, CHUNK, CHUNK) for p in parts], axis=0)
    G = pl.cdiv(flat.shape[0], LANES)
    flat = jnp.pad(flat, ((0, G * LANES - flat.shape[0]), (0, 0), (0, 0)))
    sol = _tri_solve(flat.reshape(G, LANES, CHUNK, CHUNK).transpose(0, 2, 3, 1))
    sol = sol.transpose(0, 3, 1, 2).reshape(-1, CHUNK, CHUNK)
    offs = np.cumsum([0] + sizes)
    pieces = [sol[offs[i]:offs[i + 1]].reshape(parts[i].shape) for i in range(4)]
    t_f = jnp.concatenate([pieces[0], pieces[2]], axis=2)
    t_r = jnp.concatenate([jnp.flip(pieces[1], axis=(-2, -1)), jnp.flip(pieces[3], axis=(-2, -1))], axis=2)
    tmat = jnp.stack([t_f, t_r], axis=2)
    return _dn_scan(qkvn, tmat, z, zm, gp, norm_w, n_lat_rows)


def _out_proj_kernel(p0_ref, p1_ref, p2_ref, p3_ref, w_ref, x_ref, mt_ref, nw_ref, rw_ref,
                     xo_ref, h2_ref, at_ref):
    D = x_ref.shape[2]
    o = None
    for g, p_ref in enumerate((p0_ref, p1_ref, p2_ref, p3_ref)):
        t = jnp.dot(p_ref[0], w_ref[g * GROUP_WIDTH:(g + 1) * GROUP_WIDTH, :], preferred_element_type=F32)
        o = t if o is None else o + t
    xn = x_ref[0] + mt_ref[0, 0, 2:3, :] * o
    xo_ref[0] = xn
    y = xn * lax.rsqrt(jnp.mean(xn * xn, axis=-1, keepdims=True) + RMS_EPS) * nw_ref[...]
    h2 = y * (1.0 + mt_ref[0, 0, 4:5, :]) + mt_ref[0, 0, 3:4, :]
    logits = _dot_hi(h2, rw_ref[...])
    valid = _iota(logits.shape, 1) < N_EXPERTS
    lg = jnp.where(valid, logits, -jnp.inf)
    e = jnp.exp(lg - jnp.max(lg, axis=-1, keepdims=True))
    aff = e / jnp.sum(e, axis=-1, keepdims=True)
    h2_ref[0, :, :D] = h2
    h2_ref[0, :, D:] = aff
    at_ref[0] = aff.T[:N_EXPERTS, :]


def _out_proj(parts, w_out, tok, mtab, norm_w, router_w, n_lat_tiles, n_tiles):
    B, T, D = tok.shape
    pspec = pl.BlockSpec((1, ROW_TILE, GROUP_WIDTH), lambda b, m: (b, m, 0))
    xspec = pl.BlockSpec((1, ROW_TILE, D), lambda b, m: (b, m, 0))
    return pl.pallas_call(
        _out_proj_kernel,
        grid=(B, n_tiles),
        in_specs=[pspec, pspec, pspec, pspec,
                  pl.BlockSpec((D, D), lambda b, m: (0, 0)),
                  xspec,
                  pl.BlockSpec((1, 1, 8, D), lambda b, m: (b, m // n_lat_tiles, 0, 0)),
                  pl.BlockSpec((1, D), lambda b, m: (0, 0)),
                  pl.BlockSpec((D, LANES), lambda b, m: (0, 0))],
        out_specs=[xspec,
                   pl.BlockSpec((1, ROW_TILE, D + LANES), lambda b, m: (b, m, 0)),
                   pl.BlockSpec((1, N_EXPERTS, ROW_TILE), lambda b, m: (b, 0, m))],
        out_shape=[jax.ShapeDtypeStruct((B, T, D), F32),
                   jax.ShapeDtypeStruct((B, T, D + LANES), F32),
                   jax.ShapeDtypeStruct((B, N_EXPERTS, T), F32)],
        compiler_params=_cparams(("parallel", "parallel")),
        name="out_proj",
    )(*parts, w_out, tok, mtab, norm_w.reshape(1, D), router_w)


def _route_kernel(at_ref, idx_ref, rank_ref, *, lo, n, cap):
    E = N_EXPERTS
    aff = at_ref[0, :, lo:lo + n]
    bits = pltpu.bitcast(aff, jnp.int32)
    tok = _iota((E, n), 1)
    count = lambda m: jnp.sum(m.astype(F32), axis=-1, keepdims=True)
    capf = float(cap)

    def vbit(i, t):
        cand = t | jnp.left_shift(jnp.int32(1), 30 - i)
        return jnp.where(count(bits >= cand) >= capf, cand, t)

    thr = lax.fori_loop(0, 31, vbit, jnp.zeros((E, 1), jnp.int32))
    gt, eq = bits > thr, bits == thr
    need = capf - count(gt)

    def ibit(i, m):
        cand = m | jnp.left_shift(jnp.int32(1), 12 - i)
        return jnp.where(count(eq & (tok < cand)) < need, cand, m)

    m = lax.fori_loop(0, 13, ibit, jnp.zeros((E, 1), jnp.int32))
    sel = gt | (eq & (tok <= m))
    ustrict = _tri(LANES, "ut").astype(F32)
    carry = jnp.zeros((E, 1), F32)
    for t in range(n // LANES):
        sl = slice(t * LANES, (t + 1) * LANES)
        s = sel[:, sl].astype(F32)
        rank_ref[:, sl] = jnp.where(sel[:, sl], _dot(s, ustrict) + carry, -1.0)
        carry = carry + jnp.sum(s, axis=-1, keepdims=True)
    tokc = _iota((8, n), 1)
    rsel = _iota((8, n), 0)
    rn = jnp.where(rsel == 0, tokc // 64, jnp.where(rsel == 1, tokc % 64, 0)).astype(F32)

    def per_expert(e, _):
        rk = rank_ref[pl.ds(e, 1), :]
        onehot = (rk == _iota((cap, n), 0).astype(F32)).astype(F32)
        res = _dot_nt(rn, onehot)
        idx_ref[0, pl.ds(e, 1), :] = (res[0:1, :] * 64.0 + res[1:2, :]).astype(jnp.int32)
        return 0

    lax.fori_loop(0, E, per_expert, 0)


def _route(aff_t, lo, n, cap):
    B, E, T = aff_t.shape
    return pl.pallas_call(
        functools.partial(_route_kernel, lo=lo, n=n, cap=cap),
        grid=(B,),
        in_specs=[pl.BlockSpec((1, E, T), lambda b: (b, 0, 0))],
        out_specs=pl.BlockSpec((1, E, cap), lambda b: (b, 0, 0)),
        out_shape=jax.ShapeDtypeStruct((B, E, cap), jnp.int32),
        scratch_shapes=[pltpu.VMEM((E, n), F32)],
        compiler_params=_cparams(("parallel",)),
        name="route_topk",
    )(aff_t)


def _moe_kernel(idx_ref, h2_hbm, tok_hbm, gt_ref, wg_ref, wu_ref, wd_ref, out_hbm,
                xs_ref, tr_ref, sem, *, groups):
    del tok_hbm
    e = pl.program_id(0)
    D = out_hbm.shape[1]

    def gather_copies(base, r):
        src, dst = pl.ds(idx_ref[e, base + r], 1), pl.ds(r, 1)
        return (pltpu.make_async_copy(h2_hbm.at[src, :], xs_ref.at[dst, :], sem.at[0]),
                pltpu.make_async_copy(out_hbm.at[src, :], tr_ref.at[dst, :], sem.at[1]))

    def scatter_copy(base, r):
        return pltpu.make_async_copy(tr_ref.at[pl.ds(r, 1), :], out_hbm.at[pl.ds(idx_ref[e, base + r], 1), :],
                                     sem.at[2])

    for base, n, grow in groups:
        def gather_start(r, _, base=base):
            for cp in gather_copies(base, r):
                cp.start()
            return 0

        def gather_wait(r, _, base=base):
            for cp in gather_copies(base, r):
                cp.wait()
            return 0

        def scatter_start(r, _, base=base):
            scatter_copy(base, r).start()
            return 0

        def scatter_wait(r, _, base=base):
            scatter_copy(base, r).wait()
            return 0

        lax.fori_loop(0, n, gather_start, 0)
        lax.fori_loop(0, n, gather_wait, 0)
        xs = xs_ref[0:n, :D].astype(_MXU)
        aff = xs_ref[0:n, D:]
        wt = jnp.sum(jnp.where(_iota(aff.shape, 1) == e, aff, 0.0), axis=-1, keepdims=True)
        hid = _silu(jnp.dot(xs, wg_ref[0], preferred_element_type=F32)) * \
            jnp.dot(xs, wu_ref[0], preferred_element_type=F32)
        y = jnp.dot(hid.astype(_MXU), wd_ref[0], preferred_element_type=F32) * wt
        tr_ref[0:n, :] = tr_ref[0:n, :] + gt_ref[grow:grow + 1, :] * y
        lax.fori_loop(0, n, scatter_start, 0)
        lax.fori_loop(0, n, scatter_wait, 0)


def _moe(idx, h2ext, tok, gtab, wg, wu, wd, groups):
    E, R = idx.shape
    BT, D = tok.shape
    FF = wg.shape[2]
    gmax = max(n for _, n, _ in groups)
    grid_spec = pltpu.PrefetchScalarGridSpec(
        num_scalar_prefetch=1,
        grid=(E,),
        in_specs=[pl.BlockSpec(memory_space=pl.ANY),
                  pl.BlockSpec(memory_space=pl.ANY),
                  pl.BlockSpec((8, D), lambda e, idx: (0, 0)),
                  pl.BlockSpec((1, D, FF), lambda e, idx: (e, 0, 0)),
                  pl.BlockSpec((1, D, FF), lambda e, idx: (e, 0, 0)),
                  pl.BlockSpec((1, FF, D), lambda e, idx: (e, 0, 0))],
        out_specs=pl.BlockSpec(memory_space=pl.ANY),
        scratch_shapes=[pltpu.VMEM((gmax, D + LANES), F32), pltpu.VMEM((gmax, D), F32),
                        pltpu.SemaphoreType.DMA((3,))],
    )
    return pl.pallas_call(
        functools.partial(_moe_kernel, groups=groups),
        grid_spec=grid_spec,
        out_shape=jax.ShapeDtypeStruct((BT, D), F32),
        input_output_aliases={2: 0},
        compiler_params=_cparams(("arbitrary",)),
        name="moe_experts",
    )(idx, h2ext, tok, gtab, wg, wu, wd)


def _final_norm_kernel(x_ref, w_ref, o_ref):
    x = x_ref[0]
    o_ref[0] = x * lax.rsqrt(jnp.mean(x * x, axis=-1, keepdims=True) + RMS_EPS) * w_ref[...]


def _final_norm(tok, w, n_lat_rows):
    B, T, D = tok.shape
    spec = pl.BlockSpec((1, ROW_TILE, D), lambda b, m: (b, m, 0))
    return pl.pallas_call(
        _final_norm_kernel,
        grid=(B, n_lat_rows // ROW_TILE),
        in_specs=[spec, pl.BlockSpec((1, D), lambda b, m: (0, 0))],
        out_specs=spec,
        out_shape=jax.ShapeDtypeStruct((B, n_lat_rows, D), F32),
        compiler_params=_cparams(("parallel", "parallel")),
        name="final_norm",
    )(tok, w.reshape(1, D))


def _in_proj_columns():
    o = {}
    off = 0
    for name, w in (("gla_q", 256), ("gla_k", 256), ("gla_v", 512), ("gla_g", 512), ("gla_lr", 32),
                    ("dn_qkv", 1536), ("dn_g", 512), ("dn_a", 8), ("dn_b", 8), ("gqa_q", 512),
                    ("gqa_kv", 512), ("diff_q", 512), ("diff_k", 512), ("diff_v", 512)):
        o[name] = off
        off += w
    rng = lambda a, n: list(range(a, a + n))
    cols = []
    for h in range(N_HEADS):
        cols += rng(o["gla_q"] + GLA_DK * h, GLA_DK) + rng(o["gla_k"] + GLA_DK * h, GLA_DK)
    cols += rng(o["gla_v"], 512) + rng(o["gla_g"], 512)
    for h in range(N_HEADS):
        for part in range(3):
            cols += rng(o["dn_qkv"] + part * GROUP_WIDTH + HEAD_DIM * h, HEAD_DIM)
    cols += rng(o["dn_g"], 512) + rng(o["gqa_q"], 512) + rng(o["gqa_kv"], 512)
    cols += rng(o["diff_q"], 512) + rng(o["diff_k"], 512) + rng(o["diff_v"], 512)
    assert len(cols) == Z_WIDTH
    misc = rng(o["gla_lr"], 32) + rng(o["dn_a"], 8) + rng(o["dn_b"], 8)
    dn_cols = [c - o["dn_qkv"] for c in cols[Z_DN_QKV:Z_DN_QKV + 3 * GROUP_WIDTH]]
    return np.asarray(cols), np.asarray(misc), np.asarray(dn_cols)


def _layer_params(l, p):
    cols, misc, dn_cols = _in_proj_columns()
    D = p["w_in"].shape[1]
    w_in = p["w_in"][l]
    w_big = w_in[:, cols].astype(_MXU)
    w_misc = jnp.pad(w_in[:, misc], ((0, 0), (0, LANES - len(misc)))).astype(_MXU)
    w2, gb2 = p["gla_gate_w2"][l], p["gla_gate_b"][l]
    gw = jnp.zeros((N_HEADS, LANES, LANES), F32)
    gb = []
    for h in range(N_HEADS):
        hs = slice(h * GLA_DK, (h + 1) * GLA_DK)
        gw = gw.at[h, 0:GLA_RANK, 0:GLA_DK].set(w2[0][:, hs])
        gw = gw.at[h, GLA_RANK:2 * GLA_RANK, GLA_DK:2 * GLA_DK].set(w2[1][:, hs])
        gb.append(jnp.concatenate([gb2[0][hs], gb2[1][hs]])[None, :])
    gb = jnp.stack(gb)
    conv = p["dn_conv_w"][l][:, dn_cols].reshape(SHORT_CONV, N_HEADS, 3 * HEAD_DIM).transpose(1, 0, 2)
    conv = jnp.pad(conv, ((0, 0), (0, 8 - SHORT_CONV), (0, 0)))
    al, dt = p["dn_a_log"][l], p["dn_dt_bias"][l]
    gp = jnp.stack([al[0], al[1], dt[0], dt[1]], axis=0).T
    gp = jnp.pad(jnp.broadcast_to(gp[:, :, None], (N_HEADS, 4, LANES)), ((0, 0), (0, 4), (0, 0)))
    lam_p = jnp.pad(p["diff_lambda"][l], ((0, 4), (0, LANES - DIFF_DH)))
    router = jnp.pad(p["router_w"][l], ((0, 0), (0, LANES - N_EXPERTS)))
    return dict(w_big=w_big, w_misc=w_misc, gw=gw, gb=gb, conv=conv, gp=gp, lam_p=lam_p, router=router,
                w_out=p["w_out"][l].astype(_MXU),
                wg=p["exp_w_gate"][l].astype(_MXU), wu=p["exp_w_up"][l].astype(_MXU),
                wd=p["exp_w_down"][l].astype(_MXU))


def _mixers(z, zm, lp, p, l, tabs, n_lat_rows, need_ctx, lam_init):
    T = z.shape[1]
    gla = _gla_mixer(z, zm, lp["gw"], lp["gb"], p["gla_norm_w"][l], n_lat_rows)
    dn = _dn_mixer(z, zm, lp["conv"], lp["gp"], p["dn_norm_w"][l], n_lat_rows)
    gq, gk, dq, dk = _attn_prep(z, p["gqa_q_norm"][l], p["gqa_k_norm"][l], tabs)
    n_chunks = T // ROW_TILE
    tq = 2 * ROW_TILE
    lat = dict(tq=tq, q_lo=0, nq=n_lat_rows // tq, k_lo=0, k_hi=n_chunks)
    ctx = dict(tq=ROW_TILE, q_lo=n_lat_rows // ROW_TILE, nq=(T - n_lat_rows) // ROW_TILE,
               k_lo=n_lat_rows // ROW_TILE, k_hi=n_chunks)
    gqa = _gqa_attention(gq, gk, z, None, **lat)
    diff = _diff_attention(dq, dk, z, lp["lam_p"], p["diff_norm_w"][l], None, lam_init=lam_init, **lat)
    if need_ctx:
        gqa = _gqa_attention(gq, gk, z, gqa, **ctx)
        diff = _diff_attention(dq, dk, z, lp["lam_p"], p["diff_norm_w"][l], diff, lam_init=lam_init, **ctx)
    return gla, dn, gqa, diff


def _forward(x, c, ctx, c_ctx, p):
    B, N, D = x.shape
    n_ctx = ctx.shape[1]
    T = N + n_ctx
    L = p["mod_w"].shape[0]
    n_lat_tiles = N // ROW_TILE
    cc = jnp.concatenate([c, c_ctx[None, :], jnp.zeros((8 - B - 1, D), F32)], axis=0)
    mods = _modulation(cc, p["mod_w"], p["mod_b"])
    tok = jnp.concatenate([x, ctx], axis=1)
    tabs = _rope_tables(N)
    cap_lat = EC_CAPACITY * N // N_EXPERTS
    cap_ctx = EC_CAPACITY * n_ctx // N_EXPERTS
    for l in range(L):
        last = l == L - 1
        lam_init = 0.8 - 0.6 * math.exp(-0.3 * l)
        lp = _layer_params(l, p)
        m6 = mods[l].reshape(8, 6, D)
        mtab = jnp.stack([m6[:B], jnp.broadcast_to(m6[B:B + 1], (B, 6, D))], axis=1)
        mtab = jnp.pad(mtab, ((0, 0), (0, 0), (0, 2), (0, 0)))
        z, zm = _in_proj(tok, mtab, p["norm1_w"][l], lp["w_big"], lp["w_misc"], n_lat_tiles)
        parts = _mixers(z, zm, lp, p, l, tabs, N, not last, lam_init)
        n_tiles = n_lat_tiles if last else T // ROW_TILE
        tok, h2ext, aff_t = _out_proj(parts, lp["w_out"], tok, mtab, p["norm2_w"][l], lp["router"],
                                      n_lat_tiles, n_tiles)
        base = (jnp.arange(B, dtype=jnp.int32) * T)[None, :, None]
        idx = (_route(aff_t, 0, N, cap_lat).transpose(1, 0, 2) + base).reshape(N_EXPERTS, B * cap_lat)
        groups = [(b * cap_lat, cap_lat, b) for b in range(B)]
        if not last:
            ic = _route(aff_t, N, n_ctx, cap_ctx).transpose(1, 0, 2) + base + N
            idx = jnp.concatenate([idx, ic.reshape(N_EXPERTS, B * cap_ctx)], axis=1)
            groups.append((B * cap_lat, B * cap_ctx, B))
        gtab = jnp.pad(m6[:B + 1, 5, :], ((0, 8 - B - 1), (0, 0)))
        tok = _moe(idx, h2ext.reshape(B * T, D + LANES), tok.reshape(B * T, D), gtab,
                   lp["wg"], lp["wu"], lp["wd"], tuple(groups)).reshape(B, T, D)
    return _final_norm(tok, p["final_norm_w"], N)


def kernel(x, c, ctx, c_ctx, mod_w, mod_b, norm1_w, norm2_w, w_in, w_out, gla_gate_w2, gla_gate_b, gla_norm_w,
           dn_conv_w, dn_a_log, dn_dt_bias, dn_norm_w, gqa_q_norm, gqa_k_norm, diff_lambda, diff_norm_w,
           router_w, exp_w_gate, exp_w_up, exp_w_down, final_norm_w):
    p = dict(mod_w=mod_w, mod_b=mod_b, norm1_w=norm1_w, norm2_w=norm2_w, w_in=w_in, w_out=w_out,
             gla_gate_w2=gla_gate_w2, gla_gate_b=gla_gate_b, gla_norm_w=gla_norm_w, dn_conv_w=dn_conv_w,
             dn_a_log=dn_a_log, dn_dt_bias=dn_dt_bias, dn_norm_w=dn_norm_w, gqa_q_norm=gqa_q_norm,
             gqa_k_norm=gqa_k_norm, diff_lambda=diff_lambda, diff_norm_w=diff_norm_w, router_w=router_w,
             exp_w_gate=exp_w_gate, exp_w_up=exp_w_up, exp_w_down=exp_w_down, final_norm_w=final_norm_w)
    return _forward(x, c, ctx, c_ctx, p)
```

```python
import functools
import math

import jax
import jax.numpy as jnp
import numpy as np
from jax import lax
from jax.experimental import pallas as pl
from jax.experimental.pallas import tpu as pltpu

F32 = jnp.float32
_MXU = jnp.bfloat16
_ACT = jnp.bfloat16

D_MODEL = 2048
N_CTX = 256
GRID_W = 64
HEAD_DIM = 128
N_HEADS = 4
GROUP_WIDTH = N_HEADS * HEAD_DIM
GLA_DK = 64
GLA_RANK = 16
GLA_TAU = 16.0
SHORT_CONV = 5
GQA_KV_HEADS = 2
DIFF_DH = 64
ROPE_THETA = 10000.0
N_EXPERTS = 16
EC_CAPACITY = 2
EXPERT_FF = D_MODEL // 2
RMS_EPS = 1e-6

LANES = 128
ROW_TILE = 256
CHUNK = 64
CHUNK_UNROLL = 2
VMEM_LIMIT = 56 * 1024 * 1024

Z_GLA_QK = 0
Z_GLA_V = 512
Z_GLA_G = 1024
Z_DN_QKV = 1536
Z_DN_G = 3072
Z_GQA_Q = 3584
Z_GQA_KV = 4096
Z_DIFF_Q = 4608
Z_DIFF_K = 5120
Z_DIFF_V = 5632
Z_WIDTH = 6144
M_LR = 0
M_DNA = 32
M_DNB = 40


def _cparams(sem, vmem=VMEM_LIMIT):
    return pltpu.CompilerParams(dimension_semantics=sem, vmem_limit_bytes=vmem)


def _dot(a, b):
    return jnp.dot(a.astype(_MXU), b.astype(_MXU), preferred_element_type=F32)


def _dot_nt(a, b):
    return lax.dot_general(a.astype(_MXU), b.astype(_MXU), (((1,), (1,)), ((), ())),
                           preferred_element_type=F32)


def _dot_tn(a, b):
    return lax.dot_general(a.astype(_MXU), b.astype(_MXU), (((0,), (0,)), ((), ())),
                           preferred_element_type=F32)


def _split(a, n):
    parts = []
    r = a
    for _ in range(n - 1):
        p = r.astype(_MXU)
        parts.append(p)
        r = r - p.astype(F32)
    parts.append(r.astype(_MXU))
    return parts


def _dot_hi(a, b):
    a0, a1 = _split(a, 2)
    b0, b1 = _split(b, 2)
    return _dot(a0, b0) + (_dot(a1, b0) + _dot(a0, b1))


def _dot_exact_l(e, a, fn=_dot):
    a0, a1, a2 = _split(a, 3)
    return fn(e, a0) + (fn(e, a1) + fn(e, a2))


def _dot_exact_r(a, e, fn=_dot):
    a0, a1, a2 = _split(a, 3)
    return fn(a0, e) + (fn(a1, e) + fn(a2, e))


def _silu(x):
    return x * jax.nn.sigmoid(x)


def _iota(shape, dim):
    return lax.broadcasted_iota(jnp.int32, shape, dim)


def _mod_kernel(cc_ref, w_ref, b_ref, o_ref):
    a = _silu(cc_ref[...])
    o_ref[0] = _dot_hi(a, w_ref[0]) + b_ref[0]


def _modulation(cc, mod_w, mod_b):
    L, D, W = mod_w.shape
    tn = 1024
    return pl.pallas_call(
        _mod_kernel,
        grid=(L, W // tn),
        in_specs=[pl.BlockSpec((8, D), lambda l, j: (0, 0)),
                  pl.BlockSpec((1, D, tn), lambda l, j: (l, 0, j)),
                  pl.BlockSpec((1, 1, tn), lambda l, j: (l, 0, j))],
        out_specs=pl.BlockSpec((1, 8, tn), lambda l, j: (l, 0, j)),
        out_shape=jax.ShapeDtypeStruct((L, 8, W), F32),
        compiler_params=_cparams(("parallel", "parallel")),
        name="modulation",
    )(cc, mod_w, mod_b.reshape(L, 1, W))


def _in_proj_kernel(x_ref, mt_ref, nw_ref, wb_ref, wm_ref, z_ref, zm_ref):
    x = x_ref[0]
    y = x * lax.rsqrt(jnp.mean(x * x, axis=-1, keepdims=True) + RMS_EPS) * nw_ref[...]
    h = y * (1.0 + mt_ref[0, 0, 1:2, :]) + mt_ref[0, 0, 0:1, :]
    hb = h.astype(_MXU)
    z_ref[0] = jnp.dot(hb, wb_ref[...], preferred_element_type=F32).astype(z_ref.dtype)

    @pl.when(pl.program_id(0) == 0)
    def _():
        zm_ref[0, 0] = jnp.dot(hb, wm_ref[...], preferred_element_type=F32)

    @pl.when(pl.program_id(0) != 0)
    def _():
        zm_ref[0, 0] = jnp.zeros(zm_ref.shape[2:], F32)


def _in_proj(tok, mtab, norm_w, w_big, w_misc, n_lat_tiles):
    B, T, _ = tok.shape
    D = w_big.shape[0]
    nt = T // ROW_TILE
    nj = 2
    tn = Z_WIDTH // nj
    z, zm = pl.pallas_call(
        _in_proj_kernel,
        grid=(nj, B, nt),
        in_specs=[pl.BlockSpec((1, ROW_TILE, D), lambda j, b, m: (b, m, 0)),
                  pl.BlockSpec((1, 1, 8, D), lambda j, b, m: (b, m // n_lat_tiles, 0, 0)),
                  pl.BlockSpec((1, D), lambda j, b, m: (0, 0)),
                  pl.BlockSpec((D, tn), lambda j, b, m: (0, j)),
                  pl.BlockSpec((D, LANES), lambda j, b, m: (0, 0))],
        out_specs=[pl.BlockSpec((1, ROW_TILE, tn), lambda j, b, m: (b, m, j)),
                   pl.BlockSpec((1, 1, ROW_TILE, LANES), lambda j, b, m: (j, b, m, 0))],
        out_shape=[jax.ShapeDtypeStruct((B, T, Z_WIDTH), _ACT),
                   jax.ShapeDtypeStruct((nj, B, T, LANES), F32)],
        compiler_params=_cparams(("arbitrary", "arbitrary", "arbitrary")),
        name="in_proj",
    )(tok, mtab, norm_w.reshape(1, D), w_big, w_misc)
    return z, zm[0]


def _rope(y, cos, sin, w):
    n = y.shape[-1]
    lane = _iota(y.shape, 1)
    partner = jnp.where((lane % (2 * w)) < w, pltpu.roll(y, n - w, 1), pltpu.roll(y, w, 1))
    return y * cos + partner * sin


def _attn_prep_kernel(gq_ref, gk_ref, dq_ref, dk_ref, qn_ref, kn_ref, cg_ref, sg_ref, cd_ref, sd_ref,
                      ogq_ref, ogk_ref, odq_ref, odk_ref):
    cg, sg, cd, sd = cg_ref[...], sg_ref[...], cd_ref[...], sd_ref[...]

    def head_norm(x, w):
        return x * lax.rsqrt(jnp.mean(x * x, axis=-1, keepdims=True) + RMS_EPS) * w

    for h in range(N_HEADS):
        sl = slice(h * HEAD_DIM, (h + 1) * HEAD_DIM)
        q = head_norm(gq_ref[0, :, sl].astype(F32), qn_ref[...])
        ogq_ref[0, :, sl] = (_rope(q, cg, sg, 32) * HEAD_DIM ** -0.5).astype(ogq_ref.dtype)
        dq = _rope(dq_ref[0, :, sl].astype(F32), cd, sd, 16) * DIFF_DH ** -0.5
        odq_ref[0, :, sl] = dq.astype(odq_ref.dtype)
        odk_ref[0, :, sl] = _rope(dk_ref[0, :, sl].astype(F32), cd, sd, 16).astype(odk_ref.dtype)
    for h in range(GQA_KV_HEADS):
        sl = slice(h * HEAD_DIM, (h + 1) * HEAD_DIM)
        k = head_norm(gk_ref[0, :, sl].astype(F32), kn_ref[...])
        ogk_ref[0, :, sl] = _rope(k, cg, sg, 32).astype(ogk_ref.dtype)


def _attn_prep(z, q_norm, k_norm, tabs):
    B, T, _ = z.shape
    nt = T // ROW_TILE
    zspec = lambda w, off: pl.BlockSpec((1, ROW_TILE, w), lambda b, m: (b, m, off // w))
    tspec = pl.BlockSpec((ROW_TILE, LANES), lambda b, m: (m, 0))
    wspec = pl.BlockSpec((1, LANES), lambda b, m: (0, 0))
    ospec = lambda w: pl.BlockSpec((1, ROW_TILE, w), lambda b, m: (b, m, 0))
    return pl.pallas_call(
        _attn_prep_kernel,
        grid=(B, nt),
        in_specs=[zspec(512, Z_GQA_Q), zspec(256, Z_GQA_KV), zspec(512, Z_DIFF_Q), zspec(512, Z_DIFF_K),
                  wspec, wspec, tspec, tspec, tspec, tspec],
        out_specs=[ospec(512), ospec(256), ospec(512), ospec(512)],
        out_shape=[jax.ShapeDtypeStruct((B, T, 512), _ACT), jax.ShapeDtypeStruct((B, T, 256), _ACT),
                   jax.ShapeDtypeStruct((B, T, 512), _ACT), jax.ShapeDtypeStruct((B, T, 512), _ACT)],
        compiler_params=_cparams(("parallel", "parallel")),
        name="attn_prep",
    )(z, z, z, z, q_norm.reshape(1, LANES), k_norm.reshape(1, LANES), *tabs)


def _rope_tables(n_lat):
    t = np.arange(n_lat)
    pos_r, pos_c = t // GRID_W, t % GRID_W

    def tab(d, reps):
        half = d // 2
        inv = ROPE_THETA ** (-np.arange(0, half, 2, dtype=np.float32) / half)
        ar = pos_r[:, None].astype(np.float32) * inv
        ac = pos_c[:, None].astype(np.float32) * inv
        ar, ac = jnp.asarray(ar, F32), jnp.asarray(ac, F32)
        cos = jnp.concatenate([jnp.cos(ar), jnp.cos(ar), jnp.cos(ac), jnp.cos(ac)], axis=1)
        sin = jnp.concatenate([-jnp.sin(ar), jnp.sin(ar), -jnp.sin(ac), jnp.sin(ac)], axis=1)
        cos, sin = jnp.tile(cos, (1, reps)), jnp.tile(sin, (1, reps))
        cos = jnp.concatenate([cos, jnp.ones((N_CTX, LANES), F32)], axis=0)
        sin = jnp.concatenate([sin, jnp.zeros((N_CTX, LANES), F32)], axis=0)
        return cos, sin

    cg, sg = tab(HEAD_DIM, 1)
    cd, sd = tab(DIFF_DH, 2)
    return cg, sg, cd, sd


def _softmax_step(s, m_ref, l_ref, rows):
    m_old = m_ref[rows, :]
    m_new = jnp.maximum(m_old, jnp.max(s, axis=-1, keepdims=True))
    alpha = jnp.exp(m_old - m_new)
    ps = [jnp.exp(s[:, t * LANES:(t + 1) * LANES] - m_new) for t in range(s.shape[1] // LANES)]
    psum = ps[0]
    for pt in ps[1:]:
        psum = psum + pt
    l_ref[rows, :] = alpha * l_ref[rows, :] + psum
    m_ref[rows, :] = m_new
    return jnp.concatenate(ps, axis=1), alpha


def _key_spans(k_ref, v_ref, spans, body):
    for first, tk, cnt in spans:
        def step(j, _, first=first, tk=tk):
            ks = pl.multiple_of(first + j * tk, tk)
            body(k_ref[0, pl.ds(ks, tk), :], v_ref[0, pl.ds(ks, tk), :])
            return 0
        lax.fori_loop(0, cnt, step, 0)


def _init_softmax_state(acc_ref, m_ref, l_ref):
    acc_ref[...] = jnp.zeros_like(acc_ref)
    m_ref[...] = jnp.full_like(m_ref, -jnp.inf)
    l_ref[...] = jnp.zeros_like(l_ref)


def _gqa_kernel(q_ref, k_ref, v_ref, o_ref, acc_ref, m_ref, l_ref, *, tq, spans):
    rep = N_HEADS // GQA_KV_HEADS
    q2 = jnp.concatenate([q_ref[0, :, r * HEAD_DIM:(r + 1) * HEAD_DIM] for r in range(rep)], axis=0)
    _init_softmax_state(acc_ref, m_ref, l_ref)
    allrows = slice(0, rep * tq)

    def body(kc, vc):
        p, alpha = _softmax_step(_dot_nt(q2, kc), m_ref, l_ref, allrows)
        acc_ref[...] = alpha * acc_ref[...] + _dot(p, vc)

    _key_spans(k_ref, v_ref, spans, body)
    o = acc_ref[...] / jnp.sum(l_ref[...], axis=-1, keepdims=True)
    for r in range(rep):
        o_ref[0, :, r * HEAD_DIM:(r + 1) * HEAD_DIM] = o[r * tq:(r + 1) * tq].astype(o_ref.dtype)


def _attn_scratch(rows):
    return [pltpu.VMEM((rows, HEAD_DIM), F32), pltpu.VMEM((rows, LANES), F32), pltpu.VMEM((rows, LANES), F32)]


def _gqa_attention(gq, gk, z, *, tq, q_lo, nq, spans):
    B, T, _ = gq.shape
    rep = N_HEADS // GQA_KV_HEADS
    return pl.pallas_call(
        functools.partial(_gqa_kernel, tq=tq, spans=spans),
        grid=(B, GQA_KV_HEADS, nq),
        in_specs=[pl.BlockSpec((1, tq, rep * HEAD_DIM), lambda b, g, i: (b, q_lo + i, g)),
                  pl.BlockSpec((1, T, HEAD_DIM), lambda b, g, i: (b, 0, g)),
                  pl.BlockSpec((1, T, HEAD_DIM),
                               lambda b, g, i: (b, 0, Z_GQA_KV // HEAD_DIM + GQA_KV_HEADS + g))],
        out_specs=pl.BlockSpec((1, tq, rep * HEAD_DIM), lambda b, g, i: (b, i, g)),
        out_shape=jax.ShapeDtypeStruct((B, nq * tq, GROUP_WIDTH), _ACT),
        scratch_shapes=_attn_scratch(rep * tq),
        compiler_params=_cparams(("parallel", "parallel", "parallel")),
        name="gqa_attention",
    )(gq, gk, z)


def _diff_kernel(q_ref, k_ref, v_ref, lam_ref, nw_ref, o_ref, acc_ref, m_ref, l_ref, *, tq, spans, lam_init):
    q = q_ref[0]
    lane = _iota(q.shape, 1)
    zero = jnp.zeros_like(q)
    q1 = jnp.where(lane < DIFF_DH, q, zero)
    q2 = jnp.where(lane >= DIFF_DH, q, zero)
    _init_softmax_state(acc_ref, m_ref, l_ref)
    r1, r2 = slice(0, tq), slice(tq, 2 * tq)

    def body(kc, vc):
        p1, a1 = _softmax_step(_dot_nt(q1, kc), m_ref, l_ref, r1)
        p2, a2 = _softmax_step(_dot_nt(q2, kc), m_ref, l_ref, r2)
        pv = _dot(jnp.concatenate([p1, p2], axis=0), vc)
        acc_ref[r1, :] = a1 * acc_ref[r1, :] + pv[:tq]
        acc_ref[r2, :] = a2 * acc_ref[r2, :] + pv[tq:]

    _key_spans(k_ref, v_ref, spans, body)
    lm = lam_ref[...]
    lam = (jnp.exp(jnp.sum(lm[0:1] * lm[1:2], axis=-1, keepdims=True))
           - jnp.exp(jnp.sum(lm[2:3] * lm[3:4], axis=-1, keepdims=True)) + lam_init)
    l1 = jnp.sum(l_ref[r1, :], axis=-1, keepdims=True)
    l2 = jnp.sum(l_ref[r2, :], axis=-1, keepdims=True)
    o = acc_ref[r1, :] / l1 - lam * (acc_ref[r2, :] / l2)
    y = o * lax.rsqrt(jnp.mean(o * o, axis=-1, keepdims=True) + RMS_EPS) * nw_ref[...]
    o_ref[0] = (y * (1.0 - lam_init)).astype(o_ref.dtype)


def _diff_attention(dq, dk, z, lam_p, norm_w, *, lam_init, tq, q_lo, nq, spans):
    B, T, _ = dq.shape
    return pl.pallas_call(
        functools.partial(_diff_kernel, tq=tq, spans=spans, lam_init=lam_init),
        grid=(B, N_HEADS, nq),
        in_specs=[pl.BlockSpec((1, tq, HEAD_DIM), lambda b, h, i: (b, q_lo + i, h)),
                  pl.BlockSpec((1, T, HEAD_DIM), lambda b, h, i: (b, 0, h)),
                  pl.BlockSpec((1, T, HEAD_DIM), lambda b, h, i: (b, 0, Z_DIFF_V // HEAD_DIM + h)),
                  pl.BlockSpec((8, LANES), lambda b, h, i: (0, 0)),
                  pl.BlockSpec((1, LANES), lambda b, h, i: (0, 0))],
        out_specs=pl.BlockSpec((1, tq, HEAD_DIM), lambda b, h, i: (b, i, h)),
        out_shape=jax.ShapeDtypeStruct((B, nq * tq, GROUP_WIDTH), _ACT),
        scratch_shapes=_attn_scratch(2 * tq),
        compiler_params=_cparams(("parallel", "parallel", "parallel")),
        name="diff_attention",
    )(dq, dk, z, lam_p, norm_w.reshape(1, LANES))


def _log_sigmoid(x):
    return jnp.minimum(x, 0.0) - jnp.log(1.0 + jnp.exp(-jnp.abs(x)))


def _softplus(x):
    return jnp.maximum(x, 0.0) + jnp.log(1.0 + jnp.exp(-jnp.abs(x)))


def _tri(n, kind):
    ii, jj = _iota((n, n), 0), _iota((n, n), 1)
    return {"le": ii >= jj, "lt": ii > jj, "ue": ii <= jj, "ut": ii < jj}[kind]


def _chain_orders(n_lat, n_ctx):
    fwd = [(n_lat, n_ctx, 1), (0, n_lat, 1)]
    rev = [(n_lat + n_ctx - 1, n_ctx, -1), (n_lat - 1, n_lat, -1)]
    return fwd, rev


def _head_post(o, nw, g):
    y = o * lax.rsqrt(jnp.mean(o * o, axis=-1, keepdims=True) + RMS_EPS) * nw
    return y * _silu(g)


def _gla_kernel(qk_ref, v_ref, g_ref, zm_ref, gw_ref, gb_ref, nw_ref, o_ref,
                u_ref, dl_ref, oi_ref, qg_ref, *, n_lat, n_ctx):
    C = CHUNK
    nch = n_lat + n_ctx
    le, ue = _tri(C, "le"), _tri(C, "ue")
    ltri, utri = le.astype(F32), ue.astype(F32)
    first = _iota((C, LANES), 1) < GLA_DK
    ones = jnp.ones((C, LANES), F32)
    gw, gb = gw_ref[0], gb_ref[0]
    qscale = GLA_DK ** -0.5

    def phase1(n, _):
        rows = pl.ds(pl.multiple_of(n * C, C), C)
        qk = qk_ref[0, rows, :].astype(F32)
        v = v_ref[0, rows, :]
        la = _log_sigmoid(_dot_hi(zm_ref[0, rows, :], gw) + gb) * (1.0 / GLA_TAU)
        b = jnp.where(first, _dot_exact_l(ltri, la), _dot_exact_l(utri, la))
        btot = jnp.where(first[0:1], b[C - 1:C, :], b[0:1, :])
        ref = b[C // 2:C // 2 + 1, :]
        qksw = pltpu.roll(qk, GLA_DK, 1)
        e1, e2 = jnp.exp(b - ref), jnp.exp(ref - b)
        zero = jnp.zeros_like(qk)
        qf = jnp.where(first, qk * e1, zero) * qscale
        kf = jnp.where(first, qksw * e2, zero)
        qr = jnp.where(first, zero, qksw * e1) * qscale
        kr = jnp.where(first, zero, qk * e2)
        a = jnp.where(le, _dot_nt(qf, kf), 0.0) + jnp.where(ue, _dot_nt(qr, kr), 0.0)
        oi_ref[rows, :] = _dot(a, v)
        qg_ref[rows, :] = jnp.where(first, qk, qksw) * jnp.exp(b) * qscale
        kd = jnp.where(first, qksw, qk) * jnp.exp(btot - b)
        u_ref[n] = _dot_tn(kd, v)
        dl_ref[n] = jnp.exp(_dot_exact_r(la, ones, _dot_tn))
        return 0

    lax.fori_loop(0, nch, phase1, 0, unroll=CHUNK_UNROLL)

    fwd, rev = _chain_orders(n_lat, n_ctx)
    hf, hr = slice(0, GLA_DK), slice(GLA_DK, 2 * GLA_DK)
    state = (jnp.zeros((GLA_DK, LANES), F32), jnp.zeros((GLA_DK, LANES), F32))
    for (f0, cnt, _), (r0, _, _) in zip(fwd, rev):
        def step(t, st, f0=f0, r0=r0):
            sf, sr = st
            cf, cr = f0 + t, r0 - t
            uf, df = u_ref[cf, hf, :], dl_ref[cf, hf, :]
            ur, dr = u_ref[cr, hr, :], dl_ref[cr, hr, :]
            u_ref[cf, hf, :] = sf
            u_ref[cr, hr, :] = sr
            return df * sf + uf, dr * sr + ur
        state = lax.fori_loop(0, cnt, step, state)

    def phase3(n, _):
        rows = pl.ds(pl.multiple_of(n * C, C), C)
        o = oi_ref[rows, :] + _dot(qg_ref[rows, :], u_ref[n])
        o_ref[0, rows, :] = _head_post(o, nw_ref[...], g_ref[0, rows, :].astype(F32)).astype(o_ref.dtype)
        return 0

    lax.fori_loop(0, nch, phase3, 0, unroll=CHUNK_UNROLL)


def _gla_mixer(z, zm, gw, gb, norm_w, n_lat_rows):
    B, T, _ = z.shape
    n_lat, n_ctx = n_lat_rows // CHUNK, (T - n_lat_rows) // CHUNK
    nch = n_lat + n_ctx
    zspec = lambda off: pl.BlockSpec((1, T, LANES), lambda b, h: (b, 0, off // LANES + h))
    return pl.pallas_call(
        functools.partial(_gla_kernel, n_lat=n_lat, n_ctx=n_ctx),
        grid=(B, N_HEADS),
        in_specs=[zspec(Z_GLA_QK), zspec(Z_GLA_V), zspec(Z_GLA_G),
                  pl.BlockSpec((1, T, LANES), lambda b, h: (b, 0, 0)),
                  pl.BlockSpec((1, LANES, LANES), lambda b, h: (h, 0, 0)),
                  pl.BlockSpec((1, 1, LANES), lambda b, h: (h, 0, 0)),
                  pl.BlockSpec((1, LANES), lambda b, h: (0, 0))],
        out_specs=pl.BlockSpec((1, T, LANES), lambda b, h: (b, 0, h)),
        out_shape=jax.ShapeDtypeStruct((B, T, GROUP_WIDTH), _ACT),
        scratch_shapes=[pltpu.VMEM((nch, LANES, LANES), F32), pltpu.VMEM((nch, LANES, LANES), F32),
                        pltpu.VMEM((T, LANES), F32), pltpu.VMEM((T, LANES), F32)],
        compiler_params=_cparams(("parallel", "parallel")),
        name="gla_mixer",
    )(z, z, z, zm, gw, gb, norm_w.reshape(1, LANES))


def _dn_gates(zm, gp, h):
    C = zm.shape[0]
    lane = _iota(zm.shape, 1)
    ltri, utri = _tri(C, "le").astype(F32), _tri(C, "ue").astype(F32)
    ri = _iota((8, LANES), 0)
    li = _iota((8, LANES), 1)
    sel_rows = ((li == M_DNA + h) & (ri == 0)) | ((li == M_DNA + N_HEADS + h) & (ri == 1))
    a_rows = _dot_exact_r(zm, sel_rows.astype(F32), lambda a, e: _dot_nt(e, a))
    out = []
    for d in range(2):
        pick = lambda off: jnp.sum(jnp.where(lane == off + d * N_HEADS + h, zm, 0.0), axis=-1, keepdims=True)
        na = -jnp.exp(gp[d:d + 1, :])
        dt = gp[2 + d:3 + d, :]
        g_col = na * _softplus(pick(M_DNA) + dt)
        g_row = na[:, :C] * _softplus(a_rows[d:d + 1, :] + dt[:, :C])
        tri = ltri if d == 0 else utri
        gc_col = _dot_exact_l(tri, g_col)
        gc_row = _dot_exact_r(jnp.broadcast_to(g_row, (8, C)), tri, _dot_nt)[0:1, :]
        beta = jax.nn.sigmoid(pick(M_DNB))
        out.append((beta, gc_col, gc_row))
    return out


def _dn_decay(gc_col, gc_row, mask):
    C = gc_row.shape[1]
    diff = jnp.where(mask, gc_col[:, :C] - gc_row, 0.0)
    return jnp.where(mask, jnp.exp(diff), 0.0)


def _dn_prep_kernel(qkv_ref, zm_ref, cw_ref, gp_ref, qkvn_ref, l_ref, xp_ref, *, n_lat_rows):
    C = CHUNK
    T = qkv_ref.shape[1]
    nch = T // C
    PAD = 8
    W = 3 * HEAD_DIM
    h = pl.program_id(1)
    xp_ref[0:PAD, :] = jnp.zeros((PAD, W), F32)
    xp_ref[PAD + T:PAD + T + PAD, :] = jnp.zeros((PAD, W), F32)

    def fill(n, _):
        rows = pl.ds(pl.multiple_of(n * C, C), C)
        xp_ref[pl.ds(pl.multiple_of(PAD + n * C, 8), C), :] = qkv_ref[0, rows, :].astype(F32)
        return 0

    lax.fori_loop(0, nch, fill, 0)
    cw = cw_ref[0]
    gp = gp_ref[0]
    lt, ut = _tri(C, "lt"), _tri(C, "ut")
    tcol = _iota((C, 1), 0)
    half = SHORT_CONV // 2

    def chunk(n, _):
        r0 = pl.multiple_of(n * C, C)
        rows = pl.ds(r0, C)
        win = xp_ref[pl.ds(r0, C + 2 * PAD), :]
        seg = r0 >= n_lat_rows
        acc = jnp.zeros((C, W), F32)
        for i in range(SHORT_CONV):
            xs = win[PAD + i - half:PAD + i - half + C, :]
            if i != half:
                same = ((r0 + tcol + (i - half)) >= n_lat_rows) == seg
                xs = jnp.where(same, xs, 0.0)
            acc = acc + cw[i:i + 1, :] * xs
        y = _silu(acc)
        q, k, v = y[:, :HEAD_DIM], y[:, HEAD_DIM:2 * HEAD_DIM], y[:, 2 * HEAD_DIM:]
        qn = q * lax.rsqrt(jnp.sum(q * q, axis=-1, keepdims=True) + RMS_EPS) * HEAD_DIM ** -0.5
        kn = k * lax.rsqrt(jnp.sum(k * k, axis=-1, keepdims=True) + RMS_EPS)
        qkvn_ref[0, 0, rows, :] = jnp.concatenate([qn, kn, v], axis=1).astype(qkvn_ref.dtype)
        kk = _dot_nt(kn, kn)
        (bf, gcf, grf), (br, gcr, grr) = _dn_gates(zm_ref[0, rows, :], gp, h)
        l_ref[0, 0, 0, n] = kk * bf * _dn_decay(gcf, grf, lt)
        l_ref[0, 0, 1, n] = kk * br * _dn_decay(gcr, grr, ut)
        return 0

    lax.fori_loop(0, nch, chunk, 0, unroll=CHUNK_UNROLL)


def _dn_prep(z, zm, conv_w, gp, n_lat_rows):
    B, T, _ = z.shape
    nch = T // CHUNK
    W = 3 * HEAD_DIM
    return pl.pallas_call(
        functools.partial(_dn_prep_kernel, n_lat_rows=n_lat_rows),
        grid=(B, N_HEADS),
        in_specs=[pl.BlockSpec((1, T, W), lambda b, h: (b, 0, Z_DN_QKV // W + h)),
                  pl.BlockSpec((1, T, LANES), lambda b, h: (b, 0, 0)),
                  pl.BlockSpec((1, 8, W), lambda b, h: (h, 0, 0)),
                  pl.BlockSpec((1, 8, LANES), lambda b, h: (h, 0, 0))],
        out_specs=[pl.BlockSpec((1, 1, T, W), lambda b, h: (b, h, 0, 0)),
                   pl.BlockSpec((1, 1, 2, nch, CHUNK, CHUNK), lambda b, h: (b, h, 0, 0, 0, 0))],
        out_shape=[jax.ShapeDtypeStruct((B, N_HEADS, T, W), _ACT),
                   jax.ShapeDtypeStruct((B, N_HEADS, 2, nch, CHUNK, CHUNK), F32)],
        scratch_shapes=[pltpu.VMEM((T + 16, W), F32)],
        compiler_params=_cparams(("parallel", "parallel")),
        name="dn_prep",
    )(z, zm, conv_w, gp)


def _tri_solve_kernel(l_ref, t_ref):
    C = CHUNK
    t_ref[...] = jnp.zeros_like(t_ref)
    cidx = _iota((C, LANES), 0)

    def row(i, _):
        def blk(jb, acc):
            j0 = pl.multiple_of(jb * 8, 8)
            l8 = l_ref[0, i, pl.ds(j0, 8), :]
            for k in range(8):
                acc = acc - l8[k:k + 1, :] * t_ref[0, j0 + k]
            return acc
        acc = lax.fori_loop(0, (i + 7) // 8, blk, (cidx == i).astype(F32))
        t_ref[0, i] = acc
        return 0

    lax.fori_loop(0, C, row, 0)


def _tri_solve(lt):
    G = lt.shape[0]
    spec = pl.BlockSpec((1, CHUNK, CHUNK, LANES), lambda g: (g, 0, 0, 0))
    return pl.pallas_call(
        _tri_solve_kernel, grid=(G,), in_specs=[spec], out_specs=spec,
        out_shape=jax.ShapeDtypeStruct(lt.shape, F32),
        compiler_params=_cparams(("parallel",)),
        name="dn_tri_solve",
    )(lt)


def _dn_scan_kernel(qkvn_ref, t_ref, zm_ref, gp_ref, g_ref, nw_ref, o_ref,
                    mk_ref, ns_ref, dl_ref, u_ref, w_ref, qg_ref, at_ref, *, n_lat, n_ctx):
    C = CHUNK
    nch = n_lat + n_ctx
    h = pl.program_id(1)
    gp = gp_ref[0]
    masks = (_tri(C, "le"), _tri(C, "ue"))
    last_row = (C - 1, 0)

    def phase1(n, _):
        rows = pl.ds(pl.multiple_of(n * C, C), C)
        x = qkvn_ref[0, 0, rows, :].astype(F32)
        qn, kn, vs = x[:, :HEAD_DIM], x[:, HEAD_DIM:2 * HEAD_DIM], x[:, 2 * HEAD_DIM:]
        qk = _dot_nt(qn, kn)
        gates = _dn_gates(zm_ref[0, rows, :], gp, h)
        for d in range(2):
            beta, gc, gr = gates[d]
            egc = jnp.exp(gc)
            tm = t_ref[0, 0, d, n]
            u = _dot(tm, vs * beta)
            w = _dot(tm, kn * beta * egc)
            glast = gc[last_row[d]:last_row[d] + 1, :]
            kd = kn * jnp.exp(glast - gc)
            mk_ref[d, n] = (-_dot_tn(kd, w)).astype(mk_ref.dtype)
            ns_ref[d, n] = _dot_tn(kd, u)
            dl_ref[d, n] = jnp.broadcast_to(jnp.exp(glast), (8, LANES))
            u_ref[d, rows, :] = u
            w_ref[d, rows, :] = w.astype(w_ref.dtype)
            qg_ref[d, rows, :] = (qn * egc).astype(qg_ref.dtype)
            at_ref[d, n] = (qk * _dn_decay(gc, gr, masks[d])).astype(at_ref.dtype)
        return 0

    lax.fori_loop(0, nch, phase1, 0, unroll=CHUNK_UNROLL)

    fwd, rev = _chain_orders(n_lat, n_ctx)
    state = (jnp.zeros((HEAD_DIM, HEAD_DIM), F32), jnp.zeros((HEAD_DIM, HEAD_DIM), F32))
    for (f0, cnt, _), (r0, _, _) in zip(fwd, rev):
        def step(t, st, f0=f0, r0=r0):
            new = []
            for d, c in ((0, f0 + t), (1, r0 - t)):
                s = st[d]
                nn = ns_ref[d, c]
                ns_ref[d, c] = s
                new.append(dl_ref[d, c, 0:1, :] * s + _dot(mk_ref[d, c], s) + nn)
            return tuple(new)
        state = lax.fori_loop(0, cnt, step, state)

    def phase3(n, _):
        rows = pl.ds(pl.multiple_of(n * C, C), C)
        o = jnp.zeros((C, HEAD_DIM), F32)
        for d in range(2):
            s = ns_ref[d, n]
            ws = _dot(jnp.concatenate([w_ref[d, rows, :], qg_ref[d, rows, :]], axis=0), s)
            v_new = u_ref[d, rows, :] - ws[:C]
            o = o + ws[C:] + _dot(at_ref[d, n], v_new)
        o_ref[0, rows, :] = _head_post(o, nw_ref[...], g_ref[0, rows, :].astype(F32)).astype(o_ref.dtype)
        return 0

    lax.fori_loop(0, nch, phase3, 0, unroll=CHUNK_UNROLL)


def _dn_scan(qkvn, tmat, z, zm, gp, norm_w, n_lat_rows):
    B, T, _ = z.shape
    n_lat, n_ctx = n_lat_rows // CHUNK, (T - n_lat_rows) // CHUNK
    nch = n_lat + n_ctx
    W = 3 * HEAD_DIM
    return pl.pallas_call(
        functools.partial(_dn_scan_kernel, n_lat=n_lat, n_ctx=n_ctx),
        grid=(B, N_HEADS),
        in_specs=[pl.BlockSpec((1, 1, T, W), lambda b, h: (b, h, 0, 0)),
                  pl.BlockSpec((1, 1, 2, nch, CHUNK, CHUNK), lambda b, h: (b, h, 0, 0, 0, 0)),
                  pl.BlockSpec((1, T, LANES), lambda b, h: (b, 0, 0)),
                  pl.BlockSpec((1, 8, LANES), lambda b, h: (h, 0, 0)),
                  pl.BlockSpec((1, T, LANES), lambda b, h: (b, 0, Z_DN_G // LANES + h)),
                  pl.BlockSpec((1, LANES), lambda b, h: (0, 0))],
        out_specs=pl.BlockSpec((1, T, LANES), lambda b, h: (b, 0, h)),
        out_shape=jax.ShapeDtypeStruct((B, T, GROUP_WIDTH), _ACT),
        scratch_shapes=[pltpu.VMEM((2, nch, HEAD_DIM, HEAD_DIM), _MXU),
                        pltpu.VMEM((2, nch, HEAD_DIM, HEAD_DIM), F32),
                        pltpu.VMEM((2, nch, 8, LANES), F32),
                        pltpu.VMEM((2, T, HEAD_DIM), F32),
                        pltpu.VMEM((2, T, HEAD_DIM), _MXU),
                        pltpu.VMEM((2, T, HEAD_DIM), _MXU),
                        pltpu.VMEM((2, nch, CHUNK, CHUNK), _MXU)],
        compiler_params=_cparams(("parallel", "parallel")),
        name="dn_scan",
    )(qkvn, tmat, zm, gp, z, norm_w.reshape(1, LANES))


def _dn_mixer(z, zm, conv_w, gp, norm_w, n_lat_rows):
    B, T, _ = z.shape
    n_lat = n_lat_rows // CHUNK
    qkvn, lmat = _dn_prep(z, zm, conv_w, gp, n_lat_rows)
    parts = [lmat[:, :, 0, :n_lat], jnp.flip(lmat[:, :, 1, :n_lat], axis=(-2, -1)),
             lmat[:, :, 0, n_lat:], jnp.flip(lmat[:, :, 1, n_lat:], axis=(-2, -1))]
    sizes = [int(np.prod(p.shape[:-2])) for p in parts]
    flat = jnp.concatenate([p.reshape(-1, CHUNK, CHUNK) for p in parts], axis=0)
    G = pl.cdiv(flat.shape[0], LANES)
    flat = jnp.pad(flat, ((0, G * LANES - flat.shape[0]), (0, 0), (0, 0)))
    sol = _tri_solve(flat.reshape(G, LANES, CHUNK, CHUNK).transpose(0, 2, 3, 1))
    sol = sol.transpose(0, 3, 1, 2).reshape(-1, CHUNK, CHUNK)
    offs = np.cumsum([0] + sizes)
    pieces = [sol[offs[i]:offs[i + 1]].reshape(parts[i].shape) for i in range(4)]
    t_f = jnp.concatenate([pieces[0], pieces[2]], axis=2)
    t_r = jnp.concatenate([jnp.flip(pieces[1], axis=(-2, -1)), jnp.flip(pieces[3], axis=(-2, -1))], axis=2)
    tmat = jnp.stack([t_f, t_r], axis=2)
    return _dn_scan(qkvn, tmat, z, zm, gp, norm_w, n_lat_rows)


def _out_proj_kernel(*refs, n_lat_tiles, with_ctx):
    n_in = 4 + (2 if with_ctx else 0)
    parts = list(refs[:4])
    ctx_parts = refs[4:n_in]
    w_ref, x_ref, mt_ref, nw_ref, rw_ref, xo_ref, at_ref = refs[n_in:]
    D = x_ref.shape[2]
    is_ctx = pl.program_id(1) >= n_lat_tiles
    o = None
    for g, p_ref in enumerate(parts):
        p = p_ref[0]
        if with_ctx and g >= 2:
            p = jnp.where(is_ctx, ctx_parts[g - 2][0], p)
        t = jnp.dot(p, w_ref[g * GROUP_WIDTH:(g + 1) * GROUP_WIDTH, :], preferred_element_type=F32)
        o = t if o is None else o + t
    xn = x_ref[0] + mt_ref[0, 0, 2:3, :] * o
    xo_ref[0, :, 0:D] = xn
    y = xn * lax.rsqrt(jnp.mean(xn * xn, axis=-1, keepdims=True) + RMS_EPS) * nw_ref[...]
    h2 = y * (1.0 + mt_ref[0, 0, 4:5, :]) + mt_ref[0, 0, 3:4, :]
    logits = _dot_hi(h2, rw_ref[...])
    valid = _iota(logits.shape, 1) < N_EXPERTS
    lg = jnp.where(valid, logits, -jnp.inf)
    e = jnp.exp(lg - jnp.max(lg, axis=-1, keepdims=True))
    aff = e / jnp.sum(e, axis=-1, keepdims=True)
    xo_ref[0, :, D:2 * D] = h2
    xo_ref[0, :, 2 * D:] = aff
    at_ref[0] = aff.T[:N_EXPERTS, :]


def _out_proj(parts, ctx_parts, w_out, tok, mtab, norm_w, router_w, n_lat_tiles):
    B, T, _ = tok.shape
    D = w_out.shape[0]
    with_ctx = ctx_parts is not None
    n_tiles = T // ROW_TILE if with_ctx else n_lat_tiles
    rows = n_tiles * ROW_TILE
    full = pl.BlockSpec((1, ROW_TILE, GROUP_WIDTH), lambda b, m: (b, m, 0))
    lat = pl.BlockSpec((1, ROW_TILE, GROUP_WIDTH), lambda b, m: (b, jnp.minimum(m, n_lat_tiles - 1), 0))
    ctx = pl.BlockSpec((1, ROW_TILE, GROUP_WIDTH), lambda b, m: (b, jnp.maximum(m - n_lat_tiles, 0), 0))
    xspec = pl.BlockSpec((1, ROW_TILE, D), lambda b, m: (b, m, 0))
    in_specs = [full, full, lat, lat] + ([ctx, ctx] if with_ctx else [])
    args = list(parts) + (list(ctx_parts) if with_ctx else [])
    return pl.pallas_call(
        functools.partial(_out_proj_kernel, n_lat_tiles=n_lat_tiles, with_ctx=with_ctx),
        grid=(B, n_tiles),
        in_specs=in_specs + [pl.BlockSpec((D, D), lambda b, m: (0, 0)),
                             xspec,
                             pl.BlockSpec((1, 1, 8, D), lambda b, m: (b, m // n_lat_tiles, 0, 0)),
                             pl.BlockSpec((1, D), lambda b, m: (0, 0)),
                             pl.BlockSpec((D, LANES), lambda b, m: (0, 0))],
        out_specs=[pl.BlockSpec((1, ROW_TILE, 2 * D + LANES), lambda b, m: (b, m, 0)),
                   pl.BlockSpec((1, N_EXPERTS, ROW_TILE), lambda b, m: (b, 0, m))],
        out_shape=[jax.ShapeDtypeStruct((B, rows, 2 * D + LANES), F32),
                   jax.ShapeDtypeStruct((B, N_EXPERTS, rows), F32)],
        compiler_params=_cparams(("parallel", "parallel")),
        name="out_proj",
    )(*args, w_out, tok, mtab, norm_w.reshape(1, D), router_w)


def _route_kernel(at_ref, idx_ref, rank_ref, *, lo, n, cap):
    E = N_EXPERTS
    aff = at_ref[0, :, lo:lo + n]
    bits = pltpu.bitcast(aff, jnp.int32)
    tok = _iota((E, n), 1)
    count = lambda m: jnp.sum(m.astype(F32), axis=-1, keepdims=True)
    capf = float(cap)

    def vbit(i, t):
        cand = t | jnp.left_shift(jnp.int32(1), 30 - i)
        return jnp.where(count(bits >= cand) >= capf, cand, t)

    thr = lax.fori_loop(0, 31, vbit, jnp.zeros((E, 1), jnp.int32))
    gt, eq = bits > thr, bits == thr
    need = capf - count(gt)

    def ibit(i, m):
        cand = m | jnp.left_shift(jnp.int32(1), 12 - i)
        return jnp.where(count(eq & (tok < cand)) < need, cand, m)

    m = lax.fori_loop(0, 13, ibit, jnp.zeros((E, 1), jnp.int32))
    sel = gt | (eq & (tok <= m))
    ustrict = _tri(LANES, "ut").astype(F32)
    carry = jnp.zeros((E, 1), F32)
    for t in range(n // LANES):
        sl = slice(t * LANES, (t + 1) * LANES)
        s = sel[:, sl].astype(F32)
        rank_ref[:, sl] = jnp.where(sel[:, sl], _dot(s, ustrict) + carry, -1.0)
        carry = carry + jnp.sum(s, axis=-1, keepdims=True)
    tokc = _iota((8, n), 1)
    rsel = _iota((8, n), 0)
    rn = jnp.where(rsel == 0, tokc // 64, jnp.where(rsel == 1, tokc % 64, 0)).astype(F32)

    def per_expert(e, _):
        rk = rank_ref[pl.ds(e, 1), :]
        onehot = (rk == _iota((cap, n), 0).astype(F32)).astype(F32)
        res = _dot_nt(rn, onehot)
        idx_ref[0, pl.ds(e, 1), :] = (res[0:1, :] * 64.0 + res[1:2, :]).astype(jnp.int32)
        return 0

    lax.fori_loop(0, E, per_expert, 0)


def _route(aff_t, lo, n, cap):
    B, E, T = aff_t.shape
    return pl.pallas_call(
        functools.partial(_route_kernel, lo=lo, n=n, cap=cap),
        grid=(B,),
        in_specs=[pl.BlockSpec((1, E, T), lambda b: (b, 0, 0))],
        out_specs=pl.BlockSpec((1, E, cap), lambda b: (b, 0, 0)),
        out_shape=jax.ShapeDtypeStruct((B, E, cap), jnp.int32),
        scratch_shapes=[pltpu.VMEM((E, n), F32)],
        compiler_params=_cparams(("parallel",)),
        name="route_topk",
    )(aff_t)


def _moe_kernel(idx_ref, comb_hbm, gt_ref, wg_ref, wu_ref, wd_ref, out_hbm, buf0, buf1, buf2, sem, *, groups):
    del comb_hbm
    e = pl.program_id(0)
    D = gt_ref.shape[1]
    bufs = (buf0, buf1, buf2)
    sem_g = lambda g: sem.at[g % 3]
    sem_s = lambda g: sem.at[3 + g % 3]

    def start_gather(g):
        base, n, _ = groups[g]
        for r in range(n):
            row = pl.ds(idx_ref[e, base + r], 1)
            pltpu.make_async_copy(out_hbm.at[row, :], bufs[g % 3].at[pl.ds(r, 1), :], sem_g(g)).start()

    def start_scatter(g):
        base, n, _ = groups[g]
        for r in range(n):
            row = pl.ds(idx_ref[e, base + r], 1)
            pltpu.make_async_copy(bufs[g % 3].at[pl.ds(r, 1), pl.ds(0, D)], out_hbm.at[row, pl.ds(0, D)],
                                  sem_s(g)).start()

    def wait_gather(g):
        rows = pl.ds(0, groups[g][1])
        pltpu.make_async_copy(out_hbm.at[rows, :], bufs[g % 3].at[rows, :], sem_g(g)).wait()

    def wait_scatter(g):
        rows = pl.ds(0, groups[g][1])
        pltpu.make_async_copy(bufs[g % 3].at[rows, pl.ds(0, D)], out_hbm.at[rows, pl.ds(0, D)], sem_s(g)).wait()

    def compute(g):
        _, n, grow = groups[g]
        buf = bufs[g % 3]
        xs = buf[0:n, D:2 * D].astype(_MXU)
        aff = buf[0:n, 2 * D:]
        wt = jnp.sum(jnp.where(_iota(aff.shape, 1) == e, aff, 0.0), axis=-1, keepdims=True)
        hid = _silu(jnp.dot(xs, wg_ref[0], preferred_element_type=F32)) * \
            jnp.dot(xs, wu_ref[0], preferred_element_type=F32)
        y = jnp.dot(hid.astype(_MXU), wd_ref[0], preferred_element_type=F32) * wt
        buf[0:n, 0:D] = buf[0:n, 0:D] + gt_ref[grow:grow + 1, :] * y

    G = len(groups)
    start_gather(0)
    for g in range(G):
        wait_gather(g)
        if g >= 2:
            wait_scatter(g - 2)
        if g >= 1:
            start_scatter(g - 1)
        if g + 1 < G:
            start_gather(g + 1)
        compute(g)
    start_scatter(G - 1)
    for g in range(max(G - 2, 0), G):
        wait_scatter(g)


MOE_ROWS = 256


def _moe_groups(B, cap_lat, cap_ctx):
    groups = []
    for b in range(B):
        for off in range(0, cap_lat, MOE_ROWS):
            groups.append((b * cap_lat + off, min(MOE_ROWS, cap_lat - off), b))
    if cap_ctx:
        assert B * cap_ctx <= MOE_ROWS
        groups.append((B * cap_lat, B * cap_ctx, B))
    return tuple(groups)


def _moe(idx, comb, gtab, wg, wu, wd, groups):
    E, R = idx.shape
    BT, W = comb.shape
    D, FF = wg.shape[1], wg.shape[2]
    grid_spec = pltpu.PrefetchScalarGridSpec(
        num_scalar_prefetch=1,
        grid=(E,),
        in_specs=[pl.BlockSpec(memory_space=pl.ANY),
                  pl.BlockSpec((8, D), lambda e, idx: (0, 0)),
                  pl.BlockSpec((1, D, FF), lambda e, idx: (e, 0, 0)),
                  pl.BlockSpec((1, D, FF), lambda e, idx: (e, 0, 0)),
                  pl.BlockSpec((1, FF, D), lambda e, idx: (e, 0, 0))],
        out_specs=pl.BlockSpec(memory_space=pl.ANY),
        scratch_shapes=[pltpu.VMEM((MOE_ROWS, W), F32)] * 3 + [pltpu.SemaphoreType.DMA((6,))],
    )
    return pl.pallas_call(
        functools.partial(_moe_kernel, groups=groups),
        grid_spec=grid_spec,
        out_shape=jax.ShapeDtypeStruct((BT, W), F32),
        input_output_aliases={1: 0},
        compiler_params=_cparams(("arbitrary",)),
        name="moe_experts",
    )(idx, comb, gtab, wg, wu, wd)


def _final_norm_kernel(x_ref, w_ref, o_ref):
    x = x_ref[0]
    o_ref[0] = x * lax.rsqrt(jnp.mean(x * x, axis=-1, keepdims=True) + RMS_EPS) * w_ref[...]


def _final_norm(tok, w, n_lat_rows):
    B = tok.shape[0]
    D = w.shape[0]
    spec = pl.BlockSpec((1, ROW_TILE, D), lambda b, m: (b, m, 0))
    return pl.pallas_call(
        _final_norm_kernel,
        grid=(B, n_lat_rows // ROW_TILE),
        in_specs=[spec, pl.BlockSpec((1, D), lambda b, m: (0, 0))],
        out_specs=spec,
        out_shape=jax.ShapeDtypeStruct((B, n_lat_rows, D), F32),
        compiler_params=_cparams(("parallel", "parallel")),
        name="final_norm",
    )(tok, w.reshape(1, D))


def _in_proj_columns():
    o = {}
    off = 0
    for name, w in (("gla_q", 256), ("gla_k", 256), ("gla_v", 512), ("gla_g", 512), ("gla_lr", 32),
                    ("dn_qkv", 1536), ("dn_g", 512), ("dn_a", 8), ("dn_b", 8), ("gqa_q", 512),
                    ("gqa_kv", 512), ("diff_q", 512), ("diff_k", 512), ("diff_v", 512)):
        o[name] = off
        off += w
    rng = lambda a, n: list(range(a, a + n))
    cols = []
    for h in range(N_HEADS):
        cols += rng(o["gla_q"] + GLA_DK * h, GLA_DK) + rng(o["gla_k"] + GLA_DK * h, GLA_DK)
    cols += rng(o["gla_v"], 512) + rng(o["gla_g"], 512)
    for h in range(N_HEADS):
        for part in range(3):
            cols += rng(o["dn_qkv"] + part * GROUP_WIDTH + HEAD_DIM * h, HEAD_DIM)
    cols += rng(o["dn_g"], 512) + rng(o["gqa_q"], 512) + rng(o["gqa_kv"], 512)
    cols += rng(o["diff_q"], 512) + rng(o["diff_k"], 512) + rng(o["diff_v"], 512)
    assert len(cols) == Z_WIDTH
    misc = rng(o["gla_lr"], 32) + rng(o["dn_a"], 8) + rng(o["dn_b"], 8)
    dn_cols = [c - o["dn_qkv"] for c in cols[Z_DN_QKV:Z_DN_QKV + 3 * GROUP_WIDTH]]
    return np.asarray(cols), np.asarray(misc), np.asarray(dn_cols)


def _layer_params(l, p):
    cols, misc, dn_cols = _in_proj_columns()
    D = p["w_in"].shape[1]
    w_in = p["w_in"][l]
    w_big = w_in[:, cols].astype(_MXU)
    w_misc = jnp.pad(w_in[:, misc], ((0, 0), (0, LANES - len(misc)))).astype(_MXU)
    w2, gb2 = p["gla_gate_w2"][l], p["gla_gate_b"][l]
    gw = jnp.zeros((N_HEADS, LANES, LANES), F32)
    gb = []
    for h in range(N_HEADS):
        hs = slice(h * GLA_DK, (h + 1) * GLA_DK)
        gw = gw.at[h, 0:GLA_RANK, 0:GLA_DK].set(w2[0][:, hs])
        gw = gw.at[h, GLA_RANK:2 * GLA_RANK, GLA_DK:2 * GLA_DK].set(w2[1][:, hs])
        gb.append(jnp.concatenate([gb2[0][hs], gb2[1][hs]])[None, :])
    gb = jnp.stack(gb)
    conv = p["dn_conv_w"][l][:, dn_cols].reshape(SHORT_CONV, N_HEADS, 3 * HEAD_DIM).transpose(1, 0, 2)
    conv = jnp.pad(conv, ((0, 0), (0, 8 - SHORT_CONV), (0, 0)))
    al, dt = p["dn_a_log"][l], p["dn_dt_bias"][l]
    gp = jnp.stack([al[0], al[1], dt[0], dt[1]], axis=0).T
    gp = jnp.pad(jnp.broadcast_to(gp[:, :, None], (N_HEADS, 4, LANES)), ((0, 0), (0, 4), (0, 0)))
    lam_p = jnp.pad(p["diff_lambda"][l], ((0, 4), (0, LANES - DIFF_DH)))
    router = jnp.pad(p["router_w"][l], ((0, 0), (0, LANES - N_EXPERTS)))
    return dict(w_big=w_big, w_misc=w_misc, gw=gw, gb=gb, conv=conv, gp=gp, lam_p=lam_p, router=router,
                w_out=p["w_out"][l].astype(_MXU),
                wg=p["exp_w_gate"][l].astype(_MXU), wu=p["exp_w_up"][l].astype(_MXU),
                wd=p["exp_w_down"][l].astype(_MXU))


def _mixers(z, zm, lp, p, l, tabs, n_lat_rows, need_ctx, lam_init):
    T = z.shape[1]
    gla = _gla_mixer(z, zm, lp["gw"], lp["gb"], p["gla_norm_w"][l], n_lat_rows)
    dn = _dn_mixer(z, zm, lp["conv"], lp["gp"], p["dn_norm_w"][l], n_lat_rows)
    gq, gk, dq, dk = _attn_prep(z, p["gqa_q_norm"][l], p["gqa_k_norm"][l], tabs)
    tq = 2 * ROW_TILE
    ctx_span = (n_lat_rows, ROW_TILE, (T - n_lat_rows) // ROW_TILE)
    lat = dict(tq=tq, q_lo=0, nq=n_lat_rows // tq, spans=((0, tq, n_lat_rows // tq), ctx_span))
    ctx = dict(tq=ROW_TILE, q_lo=n_lat_rows // ROW_TILE, nq=(T - n_lat_rows) // ROW_TILE, spans=(ctx_span,))
    diff_args = (dq, dk, z, lp["lam_p"], p["diff_norm_w"][l])
    parts = (gla, dn, _gqa_attention(gq, gk, z, **lat), _diff_attention(*diff_args, lam_init=lam_init, **lat))
    ctx_parts = None
    if need_ctx:
        ctx_parts = (_gqa_attention(gq, gk, z, **ctx), _diff_attention(*diff_args, lam_init=lam_init, **ctx))
    return parts, ctx_parts


def _forward(x, c, ctx, c_ctx, p):
    B, N, D = x.shape
    n_ctx = ctx.shape[1]
    T = N + n_ctx
    L = p["mod_w"].shape[0]
    n_lat_tiles = N // ROW_TILE
    cc = jnp.concatenate([c, c_ctx[None, :], jnp.zeros((8 - B - 1, D), F32)], axis=0)
    mods = _modulation(cc, p["mod_w"], p["mod_b"])
    tok = jnp.concatenate([x, ctx], axis=1)
    tabs = _rope_tables(N)
    cap_lat = EC_CAPACITY * N // N_EXPERTS
    cap_ctx = EC_CAPACITY * n_ctx // N_EXPERTS
    for l in range(L):
        last = l == L - 1
        lam_init = 0.8 - 0.6 * math.exp(-0.3 * l)
        lp = _layer_params(l, p)
        m6 = mods[l].reshape(8, 6, D)
        mtab = jnp.stack([m6[:B], jnp.broadcast_to(m6[B:B + 1], (B, 6, D))], axis=1)
        mtab = jnp.pad(mtab, ((0, 0), (0, 0), (0, 2), (0, 0)))
        z, zm = _in_proj(tok, mtab, p["norm1_w"][l], lp["w_big"], lp["w_misc"], n_lat_tiles)
        parts, ctx_parts = _mixers(z, zm, lp, p, l, tabs, N, not last, lam_init)
        tok, aff_t = _out_proj(parts, ctx_parts, lp["w_out"], tok, mtab, p["norm2_w"][l], lp["router"],
                               n_lat_tiles)
        rows_out, W = tok.shape[1:]
        base = (jnp.arange(B, dtype=jnp.int32) * rows_out)[None, :, None]
        idx = (_route(aff_t, 0, N, cap_lat).transpose(1, 0, 2) + base).reshape(N_EXPERTS, B * cap_lat)
        if not last:
            ic = _route(aff_t, N, n_ctx, cap_ctx).transpose(1, 0, 2) + base + N
            idx = jnp.concatenate([idx, ic.reshape(N_EXPERTS, B * cap_ctx)], axis=1)
        groups = _moe_groups(B, cap_lat, 0 if last else cap_ctx)
        gtab = jnp.pad(m6[:B + 1, 5, :], ((0, 8 - B - 1), (0, 0)))
        tok = _moe(idx, tok.reshape(B * rows_out, W), gtab, lp["wg"], lp["wu"], lp["wd"],
                   groups).reshape(B, rows_out, W)
    return _final_norm(tok, p["final_norm_w"], N)


def kernel(x, c, ctx, c_ctx, mod_w, mod_b, norm1_w, norm2_w, w_in, w_out, gla_gate_w2, gla_gate_b, gla_norm_w,
           dn_conv_w, dn_a_log, dn_dt_bias, dn_norm_w, gqa_q_norm, gqa_k_norm, diff_lambda, diff_norm_w,
           router_w, exp_w_gate, exp_w_up, exp_w_down, final_norm_w):
    p = dict(mod_w=mod_w, mod_b=mod_b, norm1_w=norm1_w, norm2_w=norm2_w, w_in=w_in, w_out=w_out,
             gla_gate_w2=gla_gate_w2, gla_gate_b=gla_gate_b, gla_norm_w=gla_norm_w, dn_conv_w=dn_conv_w,
             dn_a_log=dn_a_log, dn_dt_bias=dn_dt_bias, dn_norm_w=dn_norm_w, gqa_q_norm=gqa_q_norm,
             gqa_k_norm=gqa_k_norm, diff_lambda=diff_lambda, diff_norm_w=diff_norm_w, router_w=router_w,
             exp_w_gate=exp_w_gate, exp_w_up=exp_w_up, exp_w_down=exp_w_down, final_norm_w=final_norm_w)
    return _forward(x, c, ctx, c_ctx, p)
```

```python
import functools
import math

import jax
import jax.numpy as jnp
import numpy as np
from jax import lax
from jax.experimental import pallas as pl
from jax.experimental.pallas import tpu as pltpu

F32 = jnp.float32
_MXU = jnp.bfloat16
_ACT = jnp.bfloat16

D_MODEL = 2048
N_CTX = 256
GRID_W = 64
HEAD_DIM = 128
N_HEADS = 4
GROUP_WIDTH = N_HEADS * HEAD_DIM
GLA_DK = 64
GLA_RANK = 16
GLA_TAU = 16.0
SHORT_CONV = 5
GQA_KV_HEADS = 2
DIFF_DH = 64
ROPE_THETA = 10000.0
N_EXPERTS = 16
EC_CAPACITY = 2
EXPERT_FF = D_MODEL // 2
RMS_EPS = 1e-6

LANES = 128
ROW_TILE = 256
CHUNK = 64
CHUNK_UNROLL = 4
VMEM_LIMIT = 56 * 1024 * 1024

Z_GLA_QK = 0
Z_GLA_V = 512
Z_GLA_G = 1024
Z_DN_QKV = 1536
Z_DN_G = 3072
Z_GQA_Q = 3584
Z_GQA_KV = 4096
Z_DIFF_Q = 4608
Z_DIFF_K = 5120
Z_DIFF_V = 5632
Z_WIDTH = 6144
M_LR = 0
M_DNA = 32
M_DNB = 40


def _cparams(sem, vmem=VMEM_LIMIT):
    return pltpu.CompilerParams(dimension_semantics=sem, vmem_limit_bytes=vmem)


def _dot(a, b):
    return jnp.dot(a.astype(_MXU), b.astype(_MXU), preferred_element_type=F32)


def _dot_nt(a, b):
    return lax.dot_general(a.astype(_MXU), b.astype(_MXU), (((1,), (1,)), ((), ())),
                           preferred_element_type=F32)


def _dot_tn(a, b):
    return lax.dot_general(a.astype(_MXU), b.astype(_MXU), (((0,), (0,)), ((), ())),
                           preferred_element_type=F32)


def _split(a, n):
    parts = []
    r = a
    for _ in range(n - 1):
        p = r.astype(_MXU)
        parts.append(p)
        r = r - p.astype(F32)
    parts.append(r.astype(_MXU))
    return parts


def _dot_hi(a, b):
    a0, a1 = _split(a, 2)
    b0, b1 = _split(b, 2)
    return _dot(a0, b0) + (_dot(a1, b0) + _dot(a0, b1))


def _hi_lhs(a):
    a0, a1 = _split(a, 2)
    return jnp.concatenate([a0, a1, a0], axis=1)


def _hi_rhs(b):
    b0, b1 = _split(b, 2)
    return jnp.concatenate([b0, b0, b1], axis=0)


def _split_rows(a, n=3):
    return jnp.concatenate(_split(a, n), axis=0)


def _prefix_suffix(x, tri2):
    C = x.shape[0]
    out = _dot(tri2, _split_rows(x))
    return out[:C], out[C:]


def _silu(x):
    return x * jax.nn.sigmoid(x)


def _iota(shape, dim):
    return lax.broadcasted_iota(jnp.int32, shape, dim)


def _mod_kernel(cc_ref, w_ref, b_ref, o_ref):
    a = _silu(cc_ref[...])
    o_ref[0] = _dot_hi(a, w_ref[0]) + b_ref[0]


def _modulation(cc, mod_w, mod_b):
    L, D, W = mod_w.shape
    tn = 1024
    return pl.pallas_call(
        _mod_kernel,
        grid=(L, W // tn),
        in_specs=[pl.BlockSpec((8, D), lambda l, j: (0, 0)),
                  pl.BlockSpec((1, D, tn), lambda l, j: (l, 0, j)),
                  pl.BlockSpec((1, 1, tn), lambda l, j: (l, 0, j))],
        out_specs=pl.BlockSpec((1, 8, tn), lambda l, j: (l, 0, j)),
        out_shape=jax.ShapeDtypeStruct((L, 8, W), F32),
        compiler_params=_cparams(("parallel", "parallel")),
        name="modulation",
    )(cc, mod_w, mod_b.reshape(L, 1, W))


def _in_proj_kernel(x_ref, mt_ref, nw_ref, wb_ref, wm_ref, z_ref, zm_ref):
    x = x_ref[0]
    y = x * lax.rsqrt(jnp.mean(x * x, axis=-1, keepdims=True) + RMS_EPS) * nw_ref[...]
    h = y * (1.0 + mt_ref[0, 0, 1:2, :]) + mt_ref[0, 0, 0:1, :]
    hb = h.astype(_MXU)
    z_ref[0] = jnp.dot(hb, wb_ref[...], preferred_element_type=F32).astype(z_ref.dtype)

    @pl.when(pl.program_id(0) == 0)
    def _():
        zm_ref[0, 0] = jnp.dot(hb, wm_ref[...], preferred_element_type=F32)

    @pl.when(pl.program_id(0) != 0)
    def _():
        zm_ref[0, 0] = jnp.zeros(zm_ref.shape[2:], F32)


def _in_proj(tok, mtab, norm_w, w_big, w_misc, n_lat_tiles):
    B, T, _ = tok.shape
    D = w_big.shape[0]
    nt = T // ROW_TILE
    nj = 2
    tn = Z_WIDTH // nj
    z, zm = pl.pallas_call(
        _in_proj_kernel,
        grid=(nj, B, nt),
        in_specs=[pl.BlockSpec((1, ROW_TILE, D), lambda j, b, m: (b, m, 0)),
                  pl.BlockSpec((1, 1, 8, D), lambda j, b, m: (b, m // n_lat_tiles, 0, 0)),
                  pl.BlockSpec((1, D), lambda j, b, m: (0, 0)),
                  pl.BlockSpec((D, tn), lambda j, b, m: (0, j)),
                  pl.BlockSpec((D, LANES), lambda j, b, m: (0, 0))],
        out_specs=[pl.BlockSpec((1, ROW_TILE, tn), lambda j, b, m: (b, m, j)),
                   pl.BlockSpec((1, 1, ROW_TILE, LANES), lambda j, b, m: (j, b, m, 0))],
        out_shape=[jax.ShapeDtypeStruct((B, T, Z_WIDTH), _ACT),
                   jax.ShapeDtypeStruct((nj, B, T, LANES), F32)],
        compiler_params=_cparams(("arbitrary", "arbitrary", "arbitrary")),
        name="in_proj",
    )(tok, mtab, norm_w.reshape(1, D), w_big, w_misc)
    return z, zm[0]


def _rope(y, cos, sin, w):
    n = y.shape[-1]
    lane = _iota(y.shape, 1)
    partner = jnp.where((lane % (2 * w)) < w, pltpu.roll(y, n - w, 1), pltpu.roll(y, w, 1))
    return y * cos + partner * sin


def _attn_prep_kernel(gq_ref, gk_ref, dq_ref, dk_ref, qn_ref, kn_ref, cg_ref, sg_ref, cd_ref, sd_ref,
                      ogq_ref, ogk_ref, odq_ref, odk_ref):
    cg, sg, cd, sd = cg_ref[...], sg_ref[...], cd_ref[...], sd_ref[...]

    def head_norm(x, w):
        return x * lax.rsqrt(jnp.mean(x * x, axis=-1, keepdims=True) + RMS_EPS) * w

    for h in range(N_HEADS):
        sl = slice(h * HEAD_DIM, (h + 1) * HEAD_DIM)
        q = head_norm(gq_ref[0, :, sl].astype(F32), qn_ref[...])
        ogq_ref[0, :, sl] = (_rope(q, cg, sg, 32) * HEAD_DIM ** -0.5).astype(ogq_ref.dtype)
        dq = _rope(dq_ref[0, :, sl].astype(F32), cd, sd, 16) * DIFF_DH ** -0.5
        odq_ref[0, :, sl] = dq.astype(odq_ref.dtype)
        odk_ref[0, :, sl] = _rope(dk_ref[0, :, sl].astype(F32), cd, sd, 16).astype(odk_ref.dtype)
    for h in range(GQA_KV_HEADS):
        sl = slice(h * HEAD_DIM, (h + 1) * HEAD_DIM)
        k = head_norm(gk_ref[0, :, sl].astype(F32), kn_ref[...])
        ogk_ref[0, :, sl] = _rope(k, cg, sg, 32).astype(ogk_ref.dtype)


def _attn_prep(z, q_norm, k_norm, tabs):
    B, T, _ = z.shape
    nt = T // ROW_TILE
    zspec = lambda w, off: pl.BlockSpec((1, ROW_TILE, w), lambda b, m: (b, m, off // w))
    tspec = pl.BlockSpec((ROW_TILE, LANES), lambda b, m: (m, 0))
    wspec = pl.BlockSpec((1, LANES), lambda b, m: (0, 0))
    ospec = lambda w: pl.BlockSpec((1, ROW_TILE, w), lambda b, m: (b, m, 0))
    return pl.pallas_call(
        _attn_prep_kernel,
        grid=(B, nt),
        in_specs=[zspec(512, Z_GQA_Q), zspec(256, Z_GQA_KV), zspec(512, Z_DIFF_Q), zspec(512, Z_DIFF_K),
                  wspec, wspec, tspec, tspec, tspec, tspec],
        out_specs=[ospec(512), ospec(256), ospec(512), ospec(512)],
        out_shape=[jax.ShapeDtypeStruct((B, T, 512), _ACT), jax.ShapeDtypeStruct((B, T, 256), _ACT),
                   jax.ShapeDtypeStruct((B, T, 512), _ACT), jax.ShapeDtypeStruct((B, T, 512), _ACT)],
        compiler_params=_cparams(("parallel", "parallel")),
        name="attn_prep",
    )(z, z, z, z, q_norm.reshape(1, LANES), k_norm.reshape(1, LANES), *tabs)


def _rope_tables(n_lat):
    t = np.arange(n_lat)
    pos_r, pos_c = t // GRID_W, t % GRID_W

    def tab(d, reps):
        half = d // 2
        inv = ROPE_THETA ** (-np.arange(0, half, 2, dtype=np.float32) / half)
        ar = pos_r[:, None].astype(np.float32) * inv
        ac = pos_c[:, None].astype(np.float32) * inv
        ar, ac = jnp.asarray(ar, F32), jnp.asarray(ac, F32)
        cos = jnp.concatenate([jnp.cos(ar), jnp.cos(ar), jnp.cos(ac), jnp.cos(ac)], axis=1)
        sin = jnp.concatenate([-jnp.sin(ar), jnp.sin(ar), -jnp.sin(ac), jnp.sin(ac)], axis=1)
        cos, sin = jnp.tile(cos, (1, reps)), jnp.tile(sin, (1, reps))
        cos = jnp.concatenate([cos, jnp.ones((N_CTX, LANES), F32)], axis=0)
        sin = jnp.concatenate([sin, jnp.zeros((N_CTX, LANES), F32)], axis=0)
        return cos, sin

    cg, sg = tab(HEAD_DIM, 1)
    cd, sd = tab(DIFF_DH, 2)
    return cg, sg, cd, sd


def _softmax_step(s, m_ref, l_ref, rows):
    m_old = m_ref[rows, :]
    m_new = jnp.maximum(m_old, jnp.max(s, axis=-1, keepdims=True))
    alpha = jnp.exp(m_old - m_new)
    ps = [jnp.exp(s[:, t * LANES:(t + 1) * LANES] - m_new) for t in range(s.shape[1] // LANES)]
    psum = ps[0]
    for pt in ps[1:]:
        psum = psum + pt
    l_ref[rows, :] = alpha * l_ref[rows, :] + psum
    m_ref[rows, :] = m_new
    return jnp.concatenate(ps, axis=1), alpha


def _key_spans(k_ref, v_ref, spans, body):
    for first, tk, cnt in spans:
        def step(j, _, first=first, tk=tk):
            ks = pl.multiple_of(first + j * tk, tk)
            body(k_ref[0, pl.ds(ks, tk), :], v_ref[0, pl.ds(ks, tk), :])
            return 0
        lax.fori_loop(0, cnt, step, 0)


def _init_softmax_state(acc_ref, m_ref, l_ref):
    acc_ref[...] = jnp.zeros_like(acc_ref)
    m_ref[...] = jnp.full_like(m_ref, -jnp.inf)
    l_ref[...] = jnp.zeros_like(l_ref)


def _gqa_kernel(q_ref, k_ref, v_ref, o_ref, acc_ref, m_ref, l_ref, *, tq, spans):
    rep = N_HEADS // GQA_KV_HEADS
    q2 = jnp.concatenate([q_ref[0, :, r * HEAD_DIM:(r + 1) * HEAD_DIM] for r in range(rep)], axis=0)
    _init_softmax_state(acc_ref, m_ref, l_ref)
    allrows = slice(0, rep * tq)

    def body(kc, vc):
        p, alpha = _softmax_step(_dot_nt(q2, kc), m_ref, l_ref, allrows)
        acc_ref[...] = alpha * acc_ref[...] + _dot(p, vc)

    _key_spans(k_ref, v_ref, spans, body)
    o = acc_ref[...] / jnp.sum(l_ref[...], axis=-1, keepdims=True)
    for r in range(rep):
        o_ref[0, :, r * HEAD_DIM:(r + 1) * HEAD_DIM] = o[r * tq:(r + 1) * tq].astype(o_ref.dtype)


def _attn_scratch(rows):
    return [pltpu.VMEM((rows, HEAD_DIM), F32), pltpu.VMEM((rows, LANES), F32), pltpu.VMEM((rows, LANES), F32)]


def _gqa_attention(gq, gk, z, *, tq, q_lo, nq, spans):
    B, T, _ = gq.shape
    rep = N_HEADS // GQA_KV_HEADS
    return pl.pallas_call(
        functools.partial(_gqa_kernel, tq=tq, spans=spans),
        grid=(B, GQA_KV_HEADS, nq),
        in_specs=[pl.BlockSpec((1, tq, rep * HEAD_DIM), lambda b, g, i: (b, q_lo + i, g)),
                  pl.BlockSpec((1, T, HEAD_DIM), lambda b, g, i: (b, 0, g)),
                  pl.BlockSpec((1, T, HEAD_DIM),
                               lambda b, g, i: (b, 0, Z_GQA_KV // HEAD_DIM + GQA_KV_HEADS + g))],
        out_specs=pl.BlockSpec((1, tq, rep * HEAD_DIM), lambda b, g, i: (b, i, g)),
        out_shape=jax.ShapeDtypeStruct((B, nq * tq, GROUP_WIDTH), _ACT),
        scratch_shapes=_attn_scratch(rep * tq),
        compiler_params=_cparams(("parallel", "parallel", "parallel")),
        name="gqa_attention",
    )(gq, gk, z)


def _diff_kernel(q_ref, k_ref, v_ref, lam_ref, nw_ref, o_ref, acc_ref, m_ref, l_ref, *, tq, spans, lam_init):
    q = q_ref[0]
    lane = _iota(q.shape, 1)
    zero = jnp.zeros_like(q)
    q1 = jnp.where(lane < DIFF_DH, q, zero)
    q2 = jnp.where(lane >= DIFF_DH, q, zero)
    _init_softmax_state(acc_ref, m_ref, l_ref)
    r1, r2 = slice(0, tq), slice(tq, 2 * tq)

    def body(kc, vc):
        p1, a1 = _softmax_step(_dot_nt(q1, kc), m_ref, l_ref, r1)
        p2, a2 = _softmax_step(_dot_nt(q2, kc), m_ref, l_ref, r2)
        pv = _dot(jnp.concatenate([p1, p2], axis=0), vc)
        acc_ref[r1, :] = a1 * acc_ref[r1, :] + pv[:tq]
        acc_ref[r2, :] = a2 * acc_ref[r2, :] + pv[tq:]

    _key_spans(k_ref, v_ref, spans, body)
    lm = lam_ref[...]
    lam = (jnp.exp(jnp.sum(lm[0:1] * lm[1:2], axis=-1, keepdims=True))
           - jnp.exp(jnp.sum(lm[2:3] * lm[3:4], axis=-1, keepdims=True)) + lam_init)
    l1 = jnp.sum(l_ref[r1, :], axis=-1, keepdims=True)
    l2 = jnp.sum(l_ref[r2, :], axis=-1, keepdims=True)
    o = acc_ref[r1, :] / l1 - lam * (acc_ref[r2, :] / l2)
    y = o * lax.rsqrt(jnp.mean(o * o, axis=-1, keepdims=True) + RMS_EPS) * nw_ref[...]
    o_ref[0] = (y * (1.0 - lam_init)).astype(o_ref.dtype)


def _diff_attention(dq, dk, z, lam_p, norm_w, *, lam_init, tq, q_lo, nq, spans):
    B, T, _ = dq.shape
    return pl.pallas_call(
        functools.partial(_diff_kernel, tq=tq, spans=spans, lam_init=lam_init),
        grid=(B, N_HEADS, nq),
        in_specs=[pl.BlockSpec((1, tq, HEAD_DIM), lambda b, h, i: (b, q_lo + i, h)),
                  pl.BlockSpec((1, T, HEAD_DIM), lambda b, h, i: (b, 0, h)),
                  pl.BlockSpec((1, T, HEAD_DIM), lambda b, h, i: (b, 0, Z_DIFF_V // HEAD_DIM + h)),
                  pl.BlockSpec((8, LANES), lambda b, h, i: (0, 0)),
                  pl.BlockSpec((1, LANES), lambda b, h, i: (0, 0))],
        out_specs=pl.BlockSpec((1, tq, HEAD_DIM), lambda b, h, i: (b, i, h)),
        out_shape=jax.ShapeDtypeStruct((B, nq * tq, GROUP_WIDTH), _ACT),
        scratch_shapes=_attn_scratch(2 * tq),
        compiler_params=_cparams(("parallel", "parallel", "parallel")),
        name="diff_attention",
    )(dq, dk, z, lam_p, norm_w.reshape(1, LANES))


def _log_sigmoid(x):
    return jnp.minimum(x, 0.0) - jnp.log(1.0 + jnp.exp(-jnp.abs(x)))


def _softplus(x):
    return jnp.maximum(x, 0.0) + jnp.log(1.0 + jnp.exp(-jnp.abs(x)))


def _tri(n, kind, reps=1):
    assert n & (n - 1) == 0
    ii, jj = _iota((n, reps * n), 0), jnp.bitwise_and(_iota((n, reps * n), 1), n - 1)
    return {"le": ii >= jj, "lt": ii > jj, "ue": ii <= jj, "ut": ii < jj}[kind]


def _tri2(n):
    return jnp.concatenate([_tri(n, "le", 3), _tri(n, "ue", 3)], axis=0).astype(F32)


def _chain_orders(n_lat, n_ctx):
    fwd = [(n_lat, n_ctx, 1), (0, n_lat, 1)]
    rev = [(n_lat + n_ctx - 1, n_ctx, -1), (n_lat - 1, n_lat, -1)]
    return fwd, rev


def _head_post(o, nw, g):
    y = o * lax.rsqrt(jnp.mean(o * o, axis=-1, keepdims=True) + RMS_EPS) * nw
    return y * _silu(g)


def _gla_kernel(qk_ref, v_ref, g_ref, zm_ref, gw_ref, gb_ref, nw_ref, o_ref,
                u_ref, dl_ref, oi_ref, qg_ref, *, n_lat, n_ctx):
    C = CHUNK
    nch = n_lat + n_ctx
    le, ue = _tri(C, "le"), _tri(C, "ue")
    tri2 = _tri2(C)
    first = _iota((C, LANES), 1) < GLA_DK
    ones = jnp.ones((C, LANES), F32)
    gw, gb = gw_ref[0], gb_ref[0]
    qscale = GLA_DK ** -0.5

    def phase1(n, _):
        rows = pl.ds(pl.multiple_of(n * C, C), C)
        qk = qk_ref[0, rows, :].astype(F32)
        v = v_ref[0, rows, :]
        la = _log_sigmoid(_dot_hi(zm_ref[0, rows, :], gw) + gb) * (1.0 / GLA_TAU)
        pre, suf = _prefix_suffix(la, tri2)
        b = jnp.where(first, pre, suf)
        btot = jnp.where(first[0:1], b[C - 1:C, :], b[0:1, :])
        ref = b[C // 2:C // 2 + 1, :]
        qksw = pltpu.roll(qk, GLA_DK, 1)
        e1, e2 = jnp.exp(b - ref), jnp.exp(ref - b)
        zero = jnp.zeros_like(qk)
        qf = jnp.where(first, qk * e1, zero) * qscale
        kf = jnp.where(first, qksw * e2, zero)
        qr = jnp.where(first, zero, qksw * e1) * qscale
        kr = jnp.where(first, zero, qk * e2)
        a = jnp.where(le, _dot_nt(qf, kf), 0.0) + jnp.where(ue, _dot_nt(qr, kr), 0.0)
        oi_ref[rows, :] = _dot(a, v)
        qg_ref[rows, :] = jnp.where(first, qk, qksw) * jnp.exp(b) * qscale
        kd = jnp.where(first, qksw, qk) * jnp.exp(btot - b)
        u_ref[n] = _dot_tn(kd, v)
        la0, la1, la2 = _split(la, 3)
        dl_ref[n] = jnp.exp(_dot_tn(la0, ones) + (_dot_tn(la1, ones) + _dot_tn(la2, ones)))
        return 0

    lax.fori_loop(0, nch, phase1, 0, unroll=CHUNK_UNROLL)

    fwd, rev = _chain_orders(n_lat, n_ctx)
    hf, hr = slice(0, GLA_DK), slice(GLA_DK, 2 * GLA_DK)
    state = (jnp.zeros((GLA_DK, LANES), F32), jnp.zeros((GLA_DK, LANES), F32))
    for (f0, cnt, _), (r0, _, _) in zip(fwd, rev):
        def step(t, st, f0=f0, r0=r0):
            sf, sr = st
            cf, cr = f0 + t, r0 - t
            uf, df = u_ref[cf, hf, :], dl_ref[cf, hf, :]
            ur, dr = u_ref[cr, hr, :], dl_ref[cr, hr, :]
            u_ref[cf, hf, :] = sf
            u_ref[cr, hr, :] = sr
            return df * sf + uf, dr * sr + ur
        state = lax.fori_loop(0, cnt, step, state)

    def phase3(n, _):
        rows = pl.ds(pl.multiple_of(n * C, C), C)
        o = oi_ref[rows, :] + _dot(qg_ref[rows, :], u_ref[n])
        o_ref[0, rows, :] = _head_post(o, nw_ref[...], g_ref[0, rows, :].astype(F32)).astype(o_ref.dtype)
        return 0

    lax.fori_loop(0, nch, phase3, 0, unroll=CHUNK_UNROLL)


def _gla_mixer(z, zm, gw, gb, norm_w, n_lat_rows):
    B, T, _ = z.shape
    n_lat, n_ctx = n_lat_rows // CHUNK, (T - n_lat_rows) // CHUNK
    nch = n_lat + n_ctx
    zspec = lambda off: pl.BlockSpec((1, T, LANES), lambda b, h: (b, 0, off // LANES + h))
    return pl.pallas_call(
        functools.partial(_gla_kernel, n_lat=n_lat, n_ctx=n_ctx),
        grid=(B, N_HEADS),
        in_specs=[zspec(Z_GLA_QK), zspec(Z_GLA_V), zspec(Z_GLA_G),
                  pl.BlockSpec((1, T, LANES), lambda b, h: (b, 0, 0)),
                  pl.BlockSpec((1, LANES, LANES), lambda b, h: (h, 0, 0)),
                  pl.BlockSpec((1, 1, LANES), lambda b, h: (h, 0, 0)),
                  pl.BlockSpec((1, LANES), lambda b, h: (0, 0))],
        out_specs=pl.BlockSpec((1, T, LANES), lambda b, h: (b, 0, h)),
        out_shape=jax.ShapeDtypeStruct((B, T, GROUP_WIDTH), _ACT),
        scratch_shapes=[pltpu.VMEM((nch, LANES, LANES), F32), pltpu.VMEM((nch, LANES, LANES), F32),
                        pltpu.VMEM((T, LANES), F32), pltpu.VMEM((T, LANES), F32)],
        compiler_params=_cparams(("parallel", "parallel")),
        name="gla_mixer",
    )(z, z, z, zm, gw, gb, norm_w.reshape(1, LANES))


def _dn_gate_consts(C, h):
    trilu = jnp.concatenate([_tri(C, "le"), _tri(C, "ue")], axis=0).astype(_MXU)
    ri = _iota((8, 3 * LANES), 0)
    li = jnp.bitwise_and(_iota((8, 3 * LANES), 1), LANES - 1)
    sel3 = (((li == M_DNA + h) & (ri == 0)) | ((li == M_DNA + N_HEADS + h) & (ri == 1))).astype(_MXU)
    return _tri2(C).astype(_MXU), trilu, sel3


def _dn_gates(zm, gp, h, consts):
    C = zm.shape[0]
    lane = _iota(zm.shape, 1)
    first = lane < C
    tri2, trilu, sel3 = consts
    a_rows = _dot_nt(sel3, jnp.concatenate(_split(zm, 3), axis=1))
    pick = lambda off, d: jnp.sum(jnp.where(lane == off + d * N_HEADS + h, zm, 0.0), axis=-1, keepdims=True)
    g_cols, g_rows = [], []
    for d in range(2):
        na = -jnp.exp(gp[d:d + 1, :])
        dt = gp[2 + d:3 + d, :]
        g_cols.append(na * _softplus(pick(M_DNA, d) + dt))
        g_rows.append(na[:, :C] * _softplus(a_rows[d:d + 1, :] + dt[:, :C]))
    pre, suf = _prefix_suffix(jnp.where(first, g_cols[0], g_cols[1]), tri2)
    gc_cols = (jnp.where(first, pre, pltpu.roll(pre, C, 1)), jnp.where(first, pltpu.roll(suf, C, 1), suf))
    rsel = _iota((8, C), 0)
    stack = jnp.zeros((8, C), F32)
    for d in range(2):
        for i, piece in enumerate(_split(g_rows[d], 3)):
            stack = jnp.where(rsel == 3 * d + i, piece.astype(F32), stack)
    rc = _dot_nt(stack, trilu)
    gc_rows = (rc[0:1, :C] + rc[1:2, :C] + rc[2:3, :C],
               pltpu.roll(rc[3:4, :] + rc[4:5, :] + rc[5:6, :], C, 1)[:, :C])
    return [(jax.nn.sigmoid(pick(M_DNB, d)), gc_cols[d], gc_rows[d]) for d in range(2)]


def _dn_decay(gc_col, gc_row, mask):
    C = gc_row.shape[1]
    diff = jnp.where(mask, gc_col[:, :C] - gc_row, 0.0)
    return jnp.where(mask, jnp.exp(diff), 0.0)


def _dn_prep_kernel(qkv_ref, zm_ref, cw_ref, gp_ref, qkvn_ref, l_ref, xp_ref, *, n_lat_rows):
    C = CHUNK
    T = qkv_ref.shape[1]
    nch = T // C
    PAD = 8
    W = 3 * HEAD_DIM
    h = pl.program_id(1)
    xp_ref[0:PAD, :] = jnp.zeros((PAD, W), F32)
    xp_ref[PAD + T:PAD + T + PAD, :] = jnp.zeros((PAD, W), F32)

    def fill(n, _):
        rows = pl.ds(pl.multiple_of(n * C, C), C)
        xp_ref[pl.ds(pl.multiple_of(PAD + n * C, 8), C), :] = qkv_ref[0, rows, :].astype(F32)
        return 0

    lax.fori_loop(0, nch, fill, 0)
    cw = cw_ref[0]
    gp = gp_ref[0]
    lt, ut = _tri(C, "lt"), _tri(C, "ut")
    consts = _dn_gate_consts(C, h)
    tcol = _iota((C, 1), 0)
    half = SHORT_CONV // 2

    def chunk(n, _):
        r0 = pl.multiple_of(n * C, C)
        rows = pl.ds(r0, C)
        win = xp_ref[pl.ds(r0, C + 2 * PAD), :]
        seg = r0 >= n_lat_rows
        acc = jnp.zeros((C, W), F32)
        for i in range(SHORT_CONV):
            xs = win[PAD + i - half:PAD + i - half + C, :]
            if i != half:
                same = ((r0 + tcol + (i - half)) >= n_lat_rows) == seg
                xs = jnp.where(same, xs, 0.0)
            acc = acc + cw[i:i + 1, :] * xs
        y = _silu(acc)
        q, k, v = y[:, :HEAD_DIM], y[:, HEAD_DIM:2 * HEAD_DIM], y[:, 2 * HEAD_DIM:]
        qn = q * lax.rsqrt(jnp.sum(q * q, axis=-1, keepdims=True) + RMS_EPS) * HEAD_DIM ** -0.5
        kn = k * lax.rsqrt(jnp.sum(k * k, axis=-1, keepdims=True) + RMS_EPS)
        qkvn_ref[0, 0, rows, :] = jnp.concatenate([qn, kn, v], axis=1).astype(qkvn_ref.dtype)
        kk = _dot_nt(kn, kn)
        (bf, gcf, grf), (br, gcr, grr) = _dn_gates(zm_ref[0, rows, :], gp, h, consts)
        l_ref[0, 0, 0, n] = kk * bf * _dn_decay(gcf, grf, lt)
        l_ref[0, 0, 1, n] = kk * br * _dn_decay(gcr, grr, ut)
        return 0

    lax.fori_loop(0, nch, chunk, 0, unroll=CHUNK_UNROLL)


def _dn_prep(z, zm, conv_w, gp, n_lat_rows):
    B, T, _ = z.shape
    nch = T // CHUNK
    W = 3 * HEAD_DIM
    return pl.pallas_call(
        functools.partial(_dn_prep_kernel, n_lat_rows=n_lat_rows),
        grid=(B, N_HEADS),
        in_specs=[pl.BlockSpec((1, T, W), lambda b, h: (b, 0, Z_DN_QKV // W + h)),
                  pl.BlockSpec((1, T, LANES), lambda b, h: (b, 0, 0)),
                  pl.BlockSpec((1, 8, W), lambda b, h: (h, 0, 0)),
                  pl.BlockSpec((1, 8, LANES), lambda b, h: (h, 0, 0))],
        out_specs=[pl.BlockSpec((1, 1, T, W), lambda b, h: (b, h, 0, 0)),
                   pl.BlockSpec((1, 1, 2, nch, CHUNK, CHUNK), lambda b, h: (b, h, 0, 0, 0, 0))],
        out_shape=[jax.ShapeDtypeStruct((B, N_HEADS, T, W), _ACT),
                   jax.ShapeDtypeStruct((B, N_HEADS, 2, nch, CHUNK, CHUNK), F32)],
        scratch_shapes=[pltpu.VMEM((T + 16, W), F32)],
        compiler_params=_cparams(("parallel", "parallel")),
        name="dn_prep",
    )(z, zm, conv_w, gp)


def _tri_solve_kernel(l_ref, t_ref, *, n_lower):
    C = CHUNK
    t_ref[...] = jnp.zeros_like(t_ref)
    cidx = _iota((C, LANES), 0)

    def solve_row(i, jb_lo, jb_hi):
        def blk(jb, acc):
            j0 = pl.multiple_of(jb * 8, 8)
            l8 = l_ref[0, i, pl.ds(j0, 8), :]
            for k in range(8):
                acc = acc - l8[k:k + 1, :] * t_ref[0, j0 + k]
            return acc
        t_ref[0, i] = lax.fori_loop(jb_lo, jb_hi, blk, (cidx == i).astype(F32))

    @pl.when(pl.program_id(0) < n_lower)
    def _():
        def row(i, _):
            solve_row(i, 0, (i + 7) // 8)
            return 0
        lax.fori_loop(0, C, row, 0)

    @pl.when(pl.program_id(0) >= n_lower)
    def _():
        def row(t, _):
            i = C - 1 - t
            solve_row(i, i // 8, C // 8)
            return 0
        lax.fori_loop(0, C, row, 0)


def _tri_solve(lt, n_lower):
    G = lt.shape[0]
    spec = pl.BlockSpec((1, CHUNK, CHUNK, LANES), lambda g: (g, 0, 0, 0))
    return pl.pallas_call(
        functools.partial(_tri_solve_kernel, n_lower=n_lower), grid=(G,), in_specs=[spec], out_specs=spec,
        out_shape=jax.ShapeDtypeStruct(lt.shape, F32),
        compiler_params=_cparams(("parallel",)),
        name="dn_tri_solve",
    )(lt)


def _dn_scan_kernel(qkvn_ref, t_ref, zm_ref, gp_ref, g_ref, nw_ref, o_ref,
                    mk_ref, ns_ref, dl_ref, u_ref, w_ref, qg_ref, at_ref, *, n_lat, n_ctx):
    C = CHUNK
    nch = n_lat + n_ctx
    h = pl.program_id(1)
    gp = gp_ref[0]
    masks = (_tri(C, "le"), _tri(C, "ue"))
    consts = _dn_gate_consts(C, h)
    last_row = (C - 1, 0)

    def phase1(n, _):
        rows = pl.ds(pl.multiple_of(n * C, C), C)
        x = qkvn_ref[0, 0, rows, :].astype(F32)
        qn, kn, vs = x[:, :HEAD_DIM], x[:, HEAD_DIM:2 * HEAD_DIM], x[:, 2 * HEAD_DIM:]
        qk = _dot_nt(qn, kn)
        gates = _dn_gates(zm_ref[0, rows, :], gp, h, consts)
        for d in range(2):
            beta, gc, gr = gates[d]
            egc = jnp.exp(gc)
            uw = _dot(t_ref[0, 0, d, n], jnp.concatenate([vs * beta, kn * beta * egc], axis=1))
            u, w = uw[:, :HEAD_DIM], uw[:, HEAD_DIM:]
            glast = gc[last_row[d]:last_row[d] + 1, :]
            kd = kn * jnp.exp(glast - gc)
            mn = _dot_tn(kd, uw)
            mk_ref[d, n] = (-mn[:, HEAD_DIM:]).astype(mk_ref.dtype)
            ns_ref[d, n] = mn[:, :HEAD_DIM]
            dl_ref[d, n] = jnp.broadcast_to(jnp.exp(glast), (8, LANES))
            u_ref[d, rows, :] = u
            w_ref[d, rows, :] = w.astype(w_ref.dtype)
            qg_ref[d, rows, :] = (qn * egc).astype(qg_ref.dtype)
            at_ref[d, n] = (qk * _dn_decay(gc, gr, masks[d])).astype(at_ref.dtype)
        return 0

    lax.fori_loop(0, nch, phase1, 0, unroll=CHUNK_UNROLL)

    fwd, rev = _chain_orders(n_lat, n_ctx)
    state = (jnp.zeros((HEAD_DIM, HEAD_DIM), F32), jnp.zeros((HEAD_DIM, HEAD_DIM), F32))
    for (f0, cnt, _), (r0, _, _) in zip(fwd, rev):
        def step(t, st, f0=f0, r0=r0):
            new = []
            for d, c in ((0, f0 + t), (1, r0 - t)):
                s = st[d]
                nn = ns_ref[d, c]
                ns_ref[d, c] = s
                new.append(dl_ref[d, c, 0:1, :] * s + _dot(mk_ref[d, c], s) + nn)
            return tuple(new)
        state = lax.fori_loop(0, cnt, step, state)

    def phase3(n, _):
        rows = pl.ds(pl.multiple_of(n * C, C), C)
        o = jnp.zeros((C, HEAD_DIM), F32)
        for d in range(2):
            s = ns_ref[d, n]
            ws = _dot(jnp.concatenate([w_ref[d, rows, :], qg_ref[d, rows, :]], axis=0), s)
            v_new = u_ref[d, rows, :] - ws[:C]
            o = o + ws[C:] + _dot(at_ref[d, n], v_new)
        o_ref[0, rows, :] = _head_post(o, nw_ref[...], g_ref[0, rows, :].astype(F32)).astype(o_ref.dtype)
        return 0

    lax.fori_loop(0, nch, phase3, 0, unroll=CHUNK_UNROLL)


def _dn_scan(qkvn, tmat, z, zm, gp, norm_w, n_lat_rows):
    B, T, _ = z.shape
    n_lat, n_ctx = n_lat_rows // CHUNK, (T - n_lat_rows) // CHUNK
    nch = n_lat + n_ctx
    W = 3 * HEAD_DIM
    return pl.pallas_call(
        functools.partial(_dn_scan_kernel, n_lat=n_lat, n_ctx=n_ctx),
        grid=(B, N_HEADS),
        in_specs=[pl.BlockSpec((1, 1, T, W), lambda b, h: (b, h, 0, 0)),
                  pl.BlockSpec((1, 1, 2, nch, CHUNK, CHUNK), lambda b, h: (b, h, 0, 0, 0, 0)),
                  pl.BlockSpec((1, T, LANES), lambda b, h: (b, 0, 0)),
                  pl.BlockSpec((1, 8, LANES), lambda b, h: (h, 0, 0)),
                  pl.BlockSpec((1, T, LANES), lambda b, h: (b, 0, Z_DN_G // LANES + h)),
                  pl.BlockSpec((1, LANES), lambda b, h: (0, 0))],
        out_specs=pl.BlockSpec((1, T, LANES), lambda b, h: (b, 0, h)),
        out_shape=jax.ShapeDtypeStruct((B, T, GROUP_WIDTH), _ACT),
        scratch_shapes=[pltpu.VMEM((2, nch, HEAD_DIM, HEAD_DIM), _MXU),
                        pltpu.VMEM((2, nch, HEAD_DIM, HEAD_DIM), F32),
                        pltpu.VMEM((2, nch, 8, LANES), F32),
                        pltpu.VMEM((2, T, HEAD_DIM), F32),
                        pltpu.VMEM((2, T, HEAD_DIM), _MXU),
                        pltpu.VMEM((2, T, HEAD_DIM), _MXU),
                        pltpu.VMEM((2, nch, CHUNK, CHUNK), _MXU)],
        compiler_params=_cparams(("parallel", "parallel")),
        name="dn_scan",
    )(qkvn, tmat, zm, gp, z, norm_w.reshape(1, LANES))


def _dn_mixer(z, zm, conv_w, gp, norm_w, n_lat_rows):
    B, T, _ = z.shape
    qkvn, lmat = _dn_prep(z, zm, conv_w, gp, n_lat_rows)
    n_sys = B * N_HEADS * (T // CHUNK)
    n_grp = pl.cdiv(n_sys, LANES)
    flat = lmat.transpose(2, 0, 1, 3, 4, 5).reshape(2, n_sys, CHUNK, CHUNK)
    flat = jnp.pad(flat, ((0, 0), (0, n_grp * LANES - n_sys), (0, 0), (0, 0)))
    sol = _tri_solve(flat.reshape(2 * n_grp, LANES, CHUNK, CHUNK).transpose(0, 2, 3, 1), n_grp)
    sol = sol.transpose(0, 3, 1, 2).reshape(2, n_grp * LANES, CHUNK, CHUNK)[:, :n_sys]
    tmat = sol.reshape((2,) + lmat.shape[:2] + lmat.shape[3:]).transpose(1, 2, 0, 3, 4, 5)
    return _dn_scan(qkvn, tmat, z, zm, gp, norm_w, n_lat_rows)


def _out_proj_kernel(*refs, n_lat_tiles, with_ctx):
    n_in = 4 + (2 if with_ctx else 0)
    parts = list(refs[:4])
    ctx_parts = refs[4:n_in]
    w_ref, x_ref, mt_ref, nw_ref, rw_ref, xo_ref, at_ref = refs[n_in:]
    D = x_ref.shape[2]
    is_ctx = pl.program_id(1) >= n_lat_tiles
    o = None
    for g, p_ref in enumerate(parts):
        p = p_ref[0]
        if with_ctx and g >= 2:
            p = jnp.where(is_ctx, ctx_parts[g - 2][0], p)
        t = jnp.dot(p, w_ref[g * GROUP_WIDTH:(g + 1) * GROUP_WIDTH, :], preferred_element_type=F32)
        o = t if o is None else o + t
    xn = x_ref[0] + mt_ref[0, 0, 2:3, :] * o
    xo_ref[0, :, 0:D] = xn
    y = xn * lax.rsqrt(jnp.mean(xn * xn, axis=-1, keepdims=True) + RMS_EPS) * nw_ref[...]
    h2 = y * (1.0 + mt_ref[0, 0, 4:5, :]) + mt_ref[0, 0, 3:4, :]
    logits = _dot_hi(h2, rw_ref[...])
    valid = _iota(logits.shape, 1) < N_EXPERTS
    lg = jnp.where(valid, logits, -jnp.inf)
    e = jnp.exp(lg - jnp.max(lg, axis=-1, keepdims=True))
    aff = e / jnp.sum(e, axis=-1, keepdims=True)
    xo_ref[0, :, D:2 * D] = h2
    xo_ref[0, :, 2 * D:] = aff
    at_ref[0] = aff.T[:N_EXPERTS, :]


def _out_proj(parts, ctx_parts, w_out, tok, mtab, norm_w, router_w, n_lat_tiles):
    B, T, _ = tok.shape
    D = w_out.shape[0]
    with_ctx = ctx_parts is not None
    n_tiles = T // ROW_TILE if with_ctx else n_lat_tiles
    rows = n_tiles * ROW_TILE
    full = pl.BlockSpec((1, ROW_TILE, GROUP_WIDTH), lambda b, m: (b, m, 0))
    lat = pl.BlockSpec((1, ROW_TILE, GROUP_WIDTH), lambda b, m: (b, jnp.minimum(m, n_lat_tiles - 1), 0))
    ctx = pl.BlockSpec((1, ROW_TILE, GROUP_WIDTH), lambda b, m: (b, jnp.maximum(m - n_lat_tiles, 0), 0))
    xspec = pl.BlockSpec((1, ROW_TILE, D), lambda b, m: (b, m, 0))
    in_specs = [full, full, lat, lat] + ([ctx, ctx] if with_ctx else [])
    args = list(parts) + (list(ctx_parts) if with_ctx else [])
    return pl.pallas_call(
        functools.partial(_out_proj_kernel, n_lat_tiles=n_lat_tiles, with_ctx=with_ctx),
        grid=(B, n_tiles),
        in_specs=in_specs + [pl.BlockSpec((D, D), lambda b, m: (0, 0)),
                             xspec,
                             pl.BlockSpec((1, 1, 8, D), lambda b, m: (b, m // n_lat_tiles, 0, 0)),
                             pl.BlockSpec((1, D), lambda b, m: (0, 0)),
                             pl.BlockSpec((D, LANES), lambda b, m: (0, 0))],
        out_specs=[pl.BlockSpec((1, ROW_TILE, 2 * D + LANES), lambda b, m: (b, m, 0)),
                   pl.BlockSpec((1, N_EXPERTS, ROW_TILE), lambda b, m: (b, 0, m))],
        out_shape=[jax.ShapeDtypeStruct((B, rows, 2 * D + LANES), F32),
                   jax.ShapeDtypeStruct((B, N_EXPERTS, rows), F32)],
        compiler_params=_cparams(("parallel", "parallel")),
        name="out_proj",
    )(*args, w_out, tok, mtab, norm_w.reshape(1, D), router_w)


def _route_kernel(at_ref, idx_ref, rank_ref, *, lo, n, cap):
    E = N_EXPERTS
    aff = at_ref[0, :, lo:lo + n]
    bits = pltpu.bitcast(aff, jnp.int32)
    tok = _iota((E, n), 1)
    count = lambda m: jnp.sum(m.astype(F32), axis=-1, keepdims=True)
    capf = float(cap)

    def vbit(i, t):
        cand = t | jnp.left_shift(jnp.int32(1), 30 - i)
        return jnp.where(count(bits >= cand) >= capf, cand, t)

    thr = lax.fori_loop(0, 31, vbit, jnp.zeros((E, 1), jnp.int32))
    gt, eq = bits > thr, bits == thr
    need = capf - count(gt)

    def ibit(i, m):
        cand = m | jnp.left_shift(jnp.int32(1), 12 - i)
        return jnp.where(count(eq & (tok < cand)) < need, cand, m)

    m = lax.fori_loop(0, 13, ibit, jnp.zeros((E, 1), jnp.int32))
    sel = gt | (eq & (tok <= m))
    ustrict = _tri(LANES, "ut").astype(F32)
    carry = jnp.zeros((E, 1), F32)
    for t in range(n // LANES):
        sl = slice(t * LANES, (t + 1) * LANES)
        s = sel[:, sl].astype(F32)
        rank_ref[:, sl] = jnp.where(sel[:, sl], _dot(s, ustrict) + carry, -1.0)
        carry = carry + jnp.sum(s, axis=-1, keepdims=True)
    tokc = _iota((8, n), 1)
    rsel = _iota((8, n), 0)
    rn = jnp.where(rsel == 0, tokc // 64, jnp.where(rsel == 1, tokc % 64, 0)).astype(F32)

    def per_expert(e, _):
        rk = rank_ref[pl.ds(e, 1), :]
        onehot = (rk == _iota((cap, n), 0).astype(F32)).astype(F32)
        res = _dot_nt(rn, onehot)
        idx_ref[0, pl.ds(e, 1), :] = (res[0:1, :] * 64.0 + res[1:2, :]).astype(jnp.int32)
        return 0

    lax.fori_loop(0, E, per_expert, 0)


def _route(aff_t, lo, n, cap):
    B, E, T = aff_t.shape
    return pl.pallas_call(
        functools.partial(_route_kernel, lo=lo, n=n, cap=cap),
        grid=(B,),
        in_specs=[pl.BlockSpec((1, E, T), lambda b: (b, 0, 0))],
        out_specs=pl.BlockSpec((1, E, cap), lambda b: (b, 0, 0)),
        out_shape=jax.ShapeDtypeStruct((B, E, cap), jnp.int32),
        scratch_shapes=[pltpu.VMEM((E, n), F32)],
        compiler_params=_cparams(("parallel",)),
        name="route_topk",
    )(aff_t)


def _moe_kernel(idx_ref, comb_hbm, gt_ref, wg_ref, wu_ref, wd_ref, out_hbm, *scratch, groups):
    del comb_hbm
    e = pl.program_id(0)
    D = gt_ref.shape[1]
    bufs, sem = scratch[:MOE_BUFS], scratch[MOE_BUFS]
    sem_g = lambda g: sem.at[g % MOE_BUFS]
    sem_s = lambda g: sem.at[MOE_BUFS + g % MOE_BUFS]

    def start_gather(g):
        base, n, _ = groups[g]
        for r in range(n):
            row = pl.ds(idx_ref[e, base + r], 1)
            pltpu.make_async_copy(out_hbm.at[row, :], bufs[g % MOE_BUFS].at[pl.ds(r, 1), :], sem_g(g)).start()

    def start_scatter(g):
        base, n, _ = groups[g]
        for r in range(n):
            row = pl.ds(idx_ref[e, base + r], 1)
            pltpu.make_async_copy(bufs[g % MOE_BUFS].at[pl.ds(r, 1), pl.ds(0, D)], out_hbm.at[row, pl.ds(0, D)],
                                  sem_s(g)).start()

    def wait_gather(g):
        rows = pl.ds(0, groups[g][1])
        pltpu.make_async_copy(out_hbm.at[rows, :], bufs[g % MOE_BUFS].at[rows, :], sem_g(g)).wait()

    def wait_scatter(g):
        rows = pl.ds(0, groups[g][1])
        pltpu.make_async_copy(bufs[g % MOE_BUFS].at[rows, pl.ds(0, D)], out_hbm.at[rows, pl.ds(0, D)], sem_s(g)).wait()

    def compute(g):
        _, n, grow = groups[g]
        buf = bufs[g % MOE_BUFS]
        xs = buf[0:n, D:2 * D].astype(_MXU)
        aff = buf[0:n, 2 * D:]
        wt = jnp.sum(jnp.where(_iota(aff.shape, 1) == e, aff, 0.0), axis=-1, keepdims=True)
        hid = _silu(jnp.dot(xs, wg_ref[0], preferred_element_type=F32)) * \
            jnp.dot(xs, wu_ref[0], preferred_element_type=F32)
        y = jnp.dot(hid.astype(_MXU), wd_ref[0], preferred_element_type=F32) * wt
        buf[0:n, 0:D] = buf[0:n, 0:D] + gt_ref[grow:grow + 1, :] * y

    G = len(groups)
    for g in range(min(2, G)):
        start_gather(g)
    for g in range(G):
        wait_gather(g)
        if g >= 2:
            wait_scatter(g - 2)
        if g + 2 < G:
            start_gather(g + 2)
        if g >= 1:
            start_scatter(g - 1)
        compute(g)
    start_scatter(G - 1)
    for g in range(max(G - 2, 0), G):
        wait_scatter(g)


MOE_ROWS = 256
MOE_BUFS = 4


def _moe_groups(B, cap_lat, cap_ctx):
    groups = []
    for b in range(B):
        for off in range(0, cap_lat, MOE_ROWS):
            groups.append((b * cap_lat + off, min(MOE_ROWS, cap_lat - off), b))
    if cap_ctx:
        assert B * cap_ctx <= MOE_ROWS
        groups.append((B * cap_lat, B * cap_ctx, B))
    return tuple(groups)


def _moe(idx, comb, gtab, wg, wu, wd, groups):
    E, R = idx.shape
    BT, W = comb.shape
    D, FF = wg.shape[1], wg.shape[2]
    grid_spec = pltpu.PrefetchScalarGridSpec(
        num_scalar_prefetch=1,
        grid=(E,),
        in_specs=[pl.BlockSpec(memory_space=pl.ANY),
                  pl.BlockSpec((8, D), lambda e, idx: (0, 0)),
                  pl.BlockSpec((1, D, FF), lambda e, idx: (e, 0, 0)),
                  pl.BlockSpec((1, D, FF), lambda e, idx: (e, 0, 0)),
                  pl.BlockSpec((1, FF, D), lambda e, idx: (e, 0, 0))],
        out_specs=pl.BlockSpec(memory_space=pl.ANY),
        scratch_shapes=[pltpu.VMEM((MOE_ROWS, W), F32)] * MOE_BUFS + [pltpu.SemaphoreType.DMA((2 * MOE_BUFS,))],
    )
    return pl.pallas_call(
        functools.partial(_moe_kernel, groups=groups),
        grid_spec=grid_spec,
        out_shape=jax.ShapeDtypeStruct((BT, W), F32),
        input_output_aliases={1: 0},
        compiler_params=_cparams(("arbitrary",)),
        name="moe_experts",
    )(idx, comb, gtab, wg, wu, wd)


def _final_norm_kernel(x_ref, w_ref, o_ref):
    x = x_ref[0]
    o_ref[0] = x * lax.rsqrt(jnp.mean(x * x, axis=-1, keepdims=True) + RMS_EPS) * w_ref[...]


def _final_norm(tok, w, n_lat_rows):
    B = tok.shape[0]
    D = w.shape[0]
    spec = pl.BlockSpec((1, ROW_TILE, D), lambda b, m: (b, m, 0))
    return pl.pallas_call(
        _final_norm_kernel,
        grid=(B, n_lat_rows // ROW_TILE),
        in_specs=[spec, pl.BlockSpec((1, D), lambda b, m: (0, 0))],
        out_specs=spec,
        out_shape=jax.ShapeDtypeStruct((B, n_lat_rows, D), F32),
        compiler_params=_cparams(("parallel", "parallel")),
        name="final_norm",
    )(tok, w.reshape(1, D))


def _in_proj_columns():
    o = {}
    off = 0
    for name, w in (("gla_q", 256), ("gla_k", 256), ("gla_v", 512), ("gla_g", 512), ("gla_lr", 32),
                    ("dn_qkv", 1536), ("dn_g", 512), ("dn_a", 8), ("dn_b", 8), ("gqa_q", 512),
                    ("gqa_kv", 512), ("diff_q", 512), ("diff_k", 512), ("diff_v", 512)):
        o[name] = off
        off += w
    rng = lambda a, n: list(range(a, a + n))
    cols = []
    for h in range(N_HEADS):
        cols += rng(o["gla_q"] + GLA_DK * h, GLA_DK) + rng(o["gla_k"] + GLA_DK * h, GLA_DK)
    cols += rng(o["gla_v"], 512) + rng(o["gla_g"], 512)
    for h in range(N_HEADS):
        for part in range(3):
            cols += rng(o["dn_qkv"] + part * GROUP_WIDTH + HEAD_DIM * h, HEAD_DIM)
    cols += rng(o["dn_g"], 512) + rng(o["gqa_q"], 512) + rng(o["gqa_kv"], 512)
    cols += rng(o["diff_q"], 512) + rng(o["diff_k"], 512) + rng(o["diff_v"], 512)
    assert len(cols) == Z_WIDTH
    misc = rng(o["gla_lr"], 32) + rng(o["dn_a"], 8) + rng(o["dn_b"], 8)
    dn_cols = [c - o["dn_qkv"] for c in cols[Z_DN_QKV:Z_DN_QKV + 3 * GROUP_WIDTH]]
    return np.asarray(cols), np.asarray(misc), np.asarray(dn_cols)


def _runs(cols):
    cuts = [0] + [i for i in range(1, len(cols)) if cols[i] != cols[i - 1] + 1] + [len(cols)]
    return [(int(cols[a]), int(cols[b - 1]) + 1) for a, b in zip(cuts[:-1], cuts[1:])]


def _layer_params(l, p):
    cols, misc, dn_cols = _in_proj_columns()
    D = p["w_in"].shape[1]
    w_in = p["w_in"][l]
    w_big = jnp.concatenate([w_in[:, a:b] for a, b in _runs(cols)], axis=1).astype(_MXU)
    w_misc = jnp.pad(w_in[:, misc], ((0, 0), (0, LANES - len(misc)))).astype(_MXU)
    w2, gb2 = p["gla_gate_w2"][l], p["gla_gate_b"][l]
    gw = jnp.zeros((N_HEADS, LANES, LANES), F32)
    gb = []
    for h in range(N_HEADS):
        hs = slice(h * GLA_DK, (h + 1) * GLA_DK)
        gw = gw.at[h, 0:GLA_RANK, 0:GLA_DK].set(w2[0][:, hs])
        gw = gw.at[h, GLA_RANK:2 * GLA_RANK, GLA_DK:2 * GLA_DK].set(w2[1][:, hs])
        gb.append(jnp.concatenate([gb2[0][hs], gb2[1][hs]])[None, :])
    gb = jnp.stack(gb)
    conv = p["dn_conv_w"][l][:, dn_cols].reshape(SHORT_CONV, N_HEADS, 3 * HEAD_DIM).transpose(1, 0, 2)
    conv = jnp.pad(conv, ((0, 0), (0, 8 - SHORT_CONV), (0, 0)))
    al, dt = p["dn_a_log"][l], p["dn_dt_bias"][l]
    gp = jnp.stack([al[0], al[1], dt[0], dt[1]], axis=0).T
    gp = jnp.pad(jnp.broadcast_to(gp[:, :, None], (N_HEADS, 4, LANES)), ((0, 0), (0, 4), (0, 0)))
    lam_p = jnp.pad(p["diff_lambda"][l], ((0, 4), (0, LANES - DIFF_DH)))
    router = jnp.pad(p["router_w"][l], ((0, 0), (0, LANES - N_EXPERTS)))
    return dict(w_big=w_big, w_misc=w_misc, gw=gw, gb=gb, conv=conv, gp=gp, lam_p=lam_p, router=router,
                w_out=p["w_out"][l].astype(_MXU),
                wg=p["exp_w_gate"][l].astype(_MXU), wu=p["exp_w_up"][l].astype(_MXU),
                wd=p["exp_w_down"][l].astype(_MXU))


def _mixers(z, zm, lp, p, l, tabs, n_lat_rows, need_ctx, lam_init):
    T = z.shape[1]
    gla = _gla_mixer(z, zm, lp["gw"], lp["gb"], p["gla_norm_w"][l], n_lat_rows)
    dn = _dn_mixer(z, zm, lp["conv"], lp["gp"], p["dn_norm_w"][l], n_lat_rows)
    gq, gk, dq, dk = _attn_prep(z, p["gqa_q_norm"][l], p["gqa_k_norm"][l], tabs)
    tq = 2 * ROW_TILE
    ctx_span = (n_lat_rows, ROW_TILE, (T - n_lat_rows) // ROW_TILE)
    lat = dict(tq=tq, q_lo=0, nq=n_lat_rows // tq, spans=((0, tq, n_lat_rows // tq), ctx_span))
    ctx = dict(tq=ROW_TILE, q_lo=n_lat_rows // ROW_TILE, nq=(T - n_lat_rows) // ROW_TILE, spans=(ctx_span,))
    diff_args = (dq, dk, z, lp["lam_p"], p["diff_norm_w"][l])
    parts = (gla, dn, _gqa_attention(gq, gk, z, **lat), _diff_attention(*diff_args, lam_init=lam_init, **lat))
    ctx_parts = None
    if need_ctx:
        ctx_parts = (_gqa_attention(gq, gk, z, **ctx), _diff_attention(*diff_args, lam_init=lam_init, **ctx))
    return parts, ctx_parts


def _forward(x, c, ctx, c_ctx, p):
    B, N, D = x.shape
    n_ctx = ctx.shape[1]
    T = N + n_ctx
    L = p["mod_w"].shape[0]
    n_lat_tiles = N // ROW_TILE
    cc = jnp.concatenate([c, c_ctx[None, :], jnp.zeros((8 - B - 1, D), F32)], axis=0)
    mods = _modulation(cc, p["mod_w"], p["mod_b"])
    tok = jnp.concatenate([x, ctx], axis=1)
    tabs = _rope_tables(N)
    cap_lat = EC_CAPACITY * N // N_EXPERTS
    cap_ctx = EC_CAPACITY * n_ctx // N_EXPERTS
    for l in range(L):
        last = l == L - 1
        lam_init = 0.8 - 0.6 * math.exp(-0.3 * l)
        lp = _layer_params(l, p)
        m6 = mods[l].reshape(8, 6, D)
        mtab = jnp.stack([m6[:B], jnp.broadcast_to(m6[B:B + 1], (B, 6, D))], axis=1)
        mtab = jnp.pad(mtab, ((0, 0), (0, 0), (0, 2), (0, 0)))
        z, zm = _in_proj(tok, mtab, p["norm1_w"][l], lp["w_big"], lp["w_misc"], n_lat_tiles)
        parts, ctx_parts = _mixers(z, zm, lp, p, l, tabs, N, not last, lam_init)
        tok, aff_t = _out_proj(parts, ctx_parts, lp["w_out"], tok, mtab, p["norm2_w"][l], lp["router"],
                               n_lat_tiles)
        rows_out, W = tok.shape[1:]
        base = (jnp.arange(B, dtype=jnp.int32) * rows_out)[None, :, None]
        idx = (_route(aff_t, 0, N, cap_lat).transpose(1, 0, 2) + base).reshape(N_EXPERTS, B * cap_lat)
        if not last:
            ic = _route(aff_t, N, n_ctx, cap_ctx).transpose(1, 0, 2) + base + N
            idx = jnp.concatenate([idx, ic.reshape(N_EXPERTS, B * cap_ctx)], axis=1)
        groups = _moe_groups(B, cap_lat, 0 if last else cap_ctx)
        gtab = jnp.pad(m6[:B + 1, 5, :], ((0, 8 - B - 1), (0, 0)))
        tok = _moe(idx, tok.reshape(B * rows_out, W), gtab, lp["wg"], lp["wu"], lp["wd"],
                   groups).reshape(B, rows_out, W)
    return _final_norm(tok, p["final_norm_w"], N)


def kernel(x, c, ctx, c_ctx, mod_w, mod_b, norm1_w, norm2_w, w_in, w_out, gla_gate_w2, gla_gate_b, gla_norm_w,
           dn_conv_w, dn_a_log, dn_dt_bias, dn_norm_w, gqa_q_norm, gqa_k_norm, diff_lambda, diff_norm_w,
           router_w, exp_w_gate, exp_w_up, exp_w_down, final_norm_w):
    p = dict(mod_w=mod_w, mod_b=mod_b, norm1_w=norm1_w, norm2_w=norm2_w, w_in=w_in, w_out=w_out,
             gla_gate_w2=gla_gate_w2, gla_gate_b=gla_gate_b, gla_norm_w=gla_norm_w, dn_conv_w=dn_conv_w,
             dn_a_log=dn_a_log, dn_dt_bias=dn_dt_bias, dn_norm_w=dn_norm_w, gqa_q_norm=gqa_q_norm,
             gqa_k_norm=gqa_k_norm, diff_lambda=diff_lambda, diff_norm_w=diff_norm_w, router_w=router_w,
             exp_w_gate=exp_w_gate, exp_w_up=exp_w_up, exp_w_down=exp_w_down, final_norm_w=final_norm_w)
    return _forward(x, c, ctx, c_ctx, p)
```

```python
import functools
import math

import jax
import jax.numpy as jnp
import numpy as np
from jax import lax
from jax.experimental import pallas as pl
from jax.experimental.pallas import tpu as pltpu

F32 = jnp.float32
_MXU = jnp.bfloat16
_ACT = jnp.bfloat16

D_MODEL = 2048
N_CTX = 256
GRID_W = 64
HEAD_DIM = 128
N_HEADS = 4
GROUP_WIDTH = N_HEADS * HEAD_DIM
GLA_DK = 64
GLA_RANK = 16
GLA_TAU = 16.0
SHORT_CONV = 5
GQA_KV_HEADS = 2
DIFF_DH = 64
ROPE_THETA = 10000.0
N_EXPERTS = 16
EC_CAPACITY = 2
EXPERT_FF = D_MODEL // 2
RMS_EPS = 1e-6

LANES = 128
ROW_TILE = 256
CHUNK = 64
CHUNK_UNROLL = 4
VMEM_LIMIT = 56 * 1024 * 1024

Z_GLA_QK = 0
Z_GLA_V = 512
Z_GLA_G = 1024
Z_DN_QKV = 1536
Z_DN_G = 3072
Z_GQA_Q = 3584
Z_GQA_KV = 4096
Z_DIFF_Q = 4608
Z_DIFF_K = 5120
Z_DIFF_V = 5632
Z_WIDTH = 6144
M_LR = 0
M_DNA = 32
M_DNB = 40


def _cparams(sem, vmem=VMEM_LIMIT):
    return pltpu.CompilerParams(dimension_semantics=sem, vmem_limit_bytes=vmem)


def _dot(a, b):
    return jnp.dot(a.astype(_MXU), b.astype(_MXU), preferred_element_type=F32)


def _dot_nt(a, b):
    return lax.dot_general(a.astype(_MXU), b.astype(_MXU), (((1,), (1,)), ((), ())),
                           preferred_element_type=F32)


def _dot_tn(a, b):
    return lax.dot_general(a.astype(_MXU), b.astype(_MXU), (((0,), (0,)), ((), ())),
                           preferred_element_type=F32)


def _split(a, n):
    parts = []
    r = a
    for _ in range(n - 1):
        p = r.astype(_MXU)
        parts.append(p)
        r = r - p.astype(F32)
    parts.append(r.astype(_MXU))
    return parts


def _dot_hi(a, b):
    a0, a1 = _split(a, 2)
    b0, b1 = _split(b, 2)
    return _dot(a0, b0) + (_dot(a1, b0) + _dot(a0, b1))


def _hi_lhs(a):
    a0, a1 = _split(a, 2)
    return jnp.concatenate([a0, a1, a0], axis=1)


def _hi_rhs(b):
    b0, b1 = _split(b, 2)
    return jnp.concatenate([b0, b0, b1], axis=0)


def _split_rows(a, n=3):
    return jnp.concatenate(_split(a, n), axis=0)


def _prefix_suffix(x, tri2):
    C = x.shape[0]
    out = _dot(tri2, _split_rows(x))
    return out[:C], out[C:]


def _silu(x):
    return x * jax.nn.sigmoid(x)


def _iota(shape, dim):
    return lax.broadcasted_iota(jnp.int32, shape, dim)


def _mod_kernel(cc_ref, w_ref, b_ref, o_ref):
    a = _silu(cc_ref[...])
    o_ref[0] = _dot_hi(a, w_ref[0]) + b_ref[0]


def _modulation(cc, mod_w, mod_b):
    L, D, W = mod_w.shape
    tn = 1024
    return pl.pallas_call(
        _mod_kernel,
        grid=(L, W // tn),
        in_specs=[pl.BlockSpec((8, D), lambda l, j: (0, 0)),
                  pl.BlockSpec((1, D, tn), lambda l, j: (l, 0, j)),
                  pl.BlockSpec((1, 1, tn), lambda l, j: (l, 0, j))],
        out_specs=pl.BlockSpec((1, 8, tn), lambda l, j: (l, 0, j)),
        out_shape=jax.ShapeDtypeStruct((L, 8, W), F32),
        compiler_params=_cparams(("parallel", "parallel")),
        name="modulation",
    )(cc, mod_w, mod_b.reshape(L, 1, W))


def _in_proj_kernel(x_ref, mt_ref, nw_ref, wb_ref, wm_ref, z_ref, zm_ref):
    x = x_ref[0]
    y = x * lax.rsqrt(jnp.mean(x * x, axis=-1, keepdims=True) + RMS_EPS) * nw_ref[...]
    h = y * (1.0 + mt_ref[0, 0, 1:2, :]) + mt_ref[0, 0, 0:1, :]
    hb = h.astype(_MXU)
    z_ref[0] = jnp.dot(hb, wb_ref[...], preferred_element_type=F32).astype(z_ref.dtype)

    @pl.when(pl.program_id(0) == 0)
    def _():
        zm_ref[0, 0] = jnp.dot(hb, wm_ref[...], preferred_element_type=F32)

    @pl.when(pl.program_id(0) != 0)
    def _():
        zm_ref[0, 0] = jnp.zeros(zm_ref.shape[2:], F32)


def _in_proj(tok, mtab, norm_w, w_big, w_misc, n_lat_tiles):
    B, T, _ = tok.shape
    D = w_big.shape[0]
    nt = T // ROW_TILE
    nj = 2
    tn = Z_WIDTH // nj
    z, zm = pl.pallas_call(
        _in_proj_kernel,
        grid=(nj, B, nt),
        in_specs=[pl.BlockSpec((1, ROW_TILE, D), lambda j, b, m: (b, m, 0)),
                  pl.BlockSpec((1, 1, 8, D), lambda j, b, m: (b, m // n_lat_tiles, 0, 0)),
                  pl.BlockSpec((1, D), lambda j, b, m: (0, 0)),
                  pl.BlockSpec((D, tn), lambda j, b, m: (0, j)),
                  pl.BlockSpec((D, LANES), lambda j, b, m: (0, 0))],
        out_specs=[pl.BlockSpec((1, ROW_TILE, tn), lambda j, b, m: (b, m, j)),
                   pl.BlockSpec((1, 1, ROW_TILE, LANES), lambda j, b, m: (j, b, m, 0))],
        out_shape=[jax.ShapeDtypeStruct((B, T, Z_WIDTH), _ACT),
                   jax.ShapeDtypeStruct((nj, B, T, LANES), F32)],
        compiler_params=_cparams(("arbitrary", "arbitrary", "arbitrary")),
        name="in_proj",
    )(tok, mtab, norm_w.reshape(1, D), w_big, w_misc)
    return z, zm[0]


def _rope(y, cos, sin, w):
    n = y.shape[-1]
    lane = _iota(y.shape, 1)
    partner = jnp.where((lane % (2 * w)) < w, pltpu.roll(y, n - w, 1), pltpu.roll(y, w, 1))
    return y * cos + partner * sin


def _attn_prep_kernel(gq_ref, gk_ref, dq_ref, dk_ref, qn_ref, kn_ref, cg_ref, sg_ref, cd_ref, sd_ref,
                      ogq_ref, ogk_ref, odq_ref, odk_ref):
    cg, sg, cd, sd = cg_ref[...], sg_ref[...], cd_ref[...], sd_ref[...]

    def head_norm(x, w):
        return x * lax.rsqrt(jnp.mean(x * x, axis=-1, keepdims=True) + RMS_EPS) * w

    for h in range(N_HEADS):
        sl = slice(h * HEAD_DIM, (h + 1) * HEAD_DIM)
        q = head_norm(gq_ref[0, :, sl].astype(F32), qn_ref[...])
        ogq_ref[0, :, sl] = (_rope(q, cg, sg, 32) * HEAD_DIM ** -0.5).astype(ogq_ref.dtype)
        dq = _rope(dq_ref[0, :, sl].astype(F32), cd, sd, 16) * DIFF_DH ** -0.5
        odq_ref[0, :, sl] = dq.astype(odq_ref.dtype)
        odk_ref[0, :, sl] = _rope(dk_ref[0, :, sl].astype(F32), cd, sd, 16).astype(odk_ref.dtype)
    for h in range(GQA_KV_HEADS):
        sl = slice(h * HEAD_DIM, (h + 1) * HEAD_DIM)
        k = head_norm(gk_ref[0, :, sl].astype(F32), kn_ref[...])
        ogk_ref[0, :, sl] = _rope(k, cg, sg, 32).astype(ogk_ref.dtype)


def _attn_prep(z, q_norm, k_norm, tabs):
    B, T, _ = z.shape
    nt = T // ROW_TILE
    zspec = lambda w, off: pl.BlockSpec((1, ROW_TILE, w), lambda b, m: (b, m, off // w))
    tspec = pl.BlockSpec((ROW_TILE, LANES), lambda b, m: (m, 0))
    wspec = pl.BlockSpec((1, LANES), lambda b, m: (0, 0))
    ospec = lambda w: pl.BlockSpec((1, ROW_TILE, w), lambda b, m: (b, m, 0))
    return pl.pallas_call(
        _attn_prep_kernel,
        grid=(B, nt),
        in_specs=[zspec(512, Z_GQA_Q), zspec(256, Z_GQA_KV), zspec(512, Z_DIFF_Q), zspec(512, Z_DIFF_K),
                  wspec, wspec, tspec, tspec, tspec, tspec],
        out_specs=[ospec(512), ospec(256), ospec(512), ospec(512)],
        out_shape=[jax.ShapeDtypeStruct((B, T, 512), _ACT), jax.ShapeDtypeStruct((B, T, 256), _ACT),
                   jax.ShapeDtypeStruct((B, T, 512), _ACT), jax.ShapeDtypeStruct((B, T, 512), _ACT)],
        compiler_params=_cparams(("parallel", "parallel")),
        name="attn_prep",
    )(z, z, z, z, q_norm.reshape(1, LANES), k_norm.reshape(1, LANES), *tabs)


def _rope_tables(n_lat):
    t = jnp.arange(n_lat)
    pos_r, pos_c = t // GRID_W, t % GRID_W

    def tab(d, reps):
        half = d // 2
        inv = ROPE_THETA ** (-jnp.arange(0, half, 2, dtype=F32) / half)
        ar = pos_r.astype(F32)[:, None] * inv
        ac = pos_c.astype(F32)[:, None] * inv
        cos = jnp.concatenate([jnp.cos(ar), jnp.cos(ar), jnp.cos(ac), jnp.cos(ac)], axis=1)
        sin = jnp.concatenate([-jnp.sin(ar), jnp.sin(ar), -jnp.sin(ac), jnp.sin(ac)], axis=1)
        cos, sin = jnp.tile(cos, (1, reps)), jnp.tile(sin, (1, reps))
        cos = jnp.concatenate([cos, jnp.ones((N_CTX, LANES), F32)], axis=0)
        sin = jnp.concatenate([sin, jnp.zeros((N_CTX, LANES), F32)], axis=0)
        return cos, sin

    cg, sg = tab(HEAD_DIM, 1)
    cd, sd = tab(DIFF_DH, 2)
    return cg, sg, cd, sd


def _softmax_step(s, m_ref, l_ref, rows):
    m_old = m_ref[rows, :]
    m_new = jnp.maximum(m_old, jnp.max(s, axis=-1, keepdims=True))
    alpha = jnp.exp(m_old - m_new)
    ps = [jnp.exp(s[:, t * LANES:(t + 1) * LANES] - m_new) for t in range(s.shape[1] // LANES)]
    psum = ps[0]
    for pt in ps[1:]:
        psum = psum + pt
    l_ref[rows, :] = alpha * l_ref[rows, :] + psum
    m_ref[rows, :] = m_new
    return jnp.concatenate(ps, axis=1), alpha


def _key_spans(k_ref, v_ref, spans, body):
    for first, tk, cnt in spans:
        def step(j, _, first=first, tk=tk):
            ks = pl.multiple_of(first + j * tk, tk)
            body(k_ref[0, pl.ds(ks, tk), :], v_ref[0, pl.ds(ks, tk), :])
            return 0
        lax.fori_loop(0, cnt, step, 0)


def _init_softmax_state(acc_ref, m_ref, l_ref):
    acc_ref[...] = jnp.zeros_like(acc_ref)
    m_ref[...] = jnp.full_like(m_ref, -jnp.inf)
    l_ref[...] = jnp.zeros_like(l_ref)


def _gqa_kernel(q_ref, k_ref, v_ref, o_ref, acc_ref, m_ref, l_ref, *, tq, spans):
    rep = N_HEADS // GQA_KV_HEADS
    q2 = jnp.concatenate([q_ref[0, :, r * HEAD_DIM:(r + 1) * HEAD_DIM] for r in range(rep)], axis=0)
    _init_softmax_state(acc_ref, m_ref, l_ref)
    allrows = slice(0, rep * tq)

    def body(kc, vc):
        p, alpha = _softmax_step(_dot_nt(q2, kc), m_ref, l_ref, allrows)
        acc_ref[...] = alpha * acc_ref[...] + _dot(p, vc)

    _key_spans(k_ref, v_ref, spans, body)
    o = acc_ref[...] / jnp.sum(l_ref[...], axis=-1, keepdims=True)
    for r in range(rep):
        o_ref[0, :, r * HEAD_DIM:(r + 1) * HEAD_DIM] = o[r * tq:(r + 1) * tq].astype(o_ref.dtype)


def _attn_scratch(rows):
    return [pltpu.VMEM((rows, HEAD_DIM), F32), pltpu.VMEM((rows, LANES), F32), pltpu.VMEM((rows, LANES), F32)]


def _gqa_attention(gq, gk, z, *, tq, q_lo, nq, spans):
    B, T, _ = gq.shape
    rep = N_HEADS // GQA_KV_HEADS
    return pl.pallas_call(
        functools.partial(_gqa_kernel, tq=tq, spans=spans),
        grid=(B, GQA_KV_HEADS, nq),
        in_specs=[pl.BlockSpec((1, tq, rep * HEAD_DIM), lambda b, g, i: (b, q_lo + i, g)),
                  pl.BlockSpec((1, T, HEAD_DIM), lambda b, g, i: (b, 0, g)),
                  pl.BlockSpec((1, T, HEAD_DIM),
                               lambda b, g, i: (b, 0, Z_GQA_KV // HEAD_DIM + GQA_KV_HEADS + g))],
        out_specs=pl.BlockSpec((1, tq, rep * HEAD_DIM), lambda b, g, i: (b, i, g)),
        out_shape=jax.ShapeDtypeStruct((B, nq * tq, GROUP_WIDTH), _ACT),
        scratch_shapes=_attn_scratch(rep * tq),
        compiler_params=_cparams(("parallel", "parallel", "parallel")),
        name="gqa_attention",
    )(gq, gk, z)


def _diff_kernel(q_ref, k_ref, v_ref, lam_ref, nw_ref, o_ref, acc_ref, m_ref, l_ref, *, tq, spans, lam_init):
    q = q_ref[0]
    lane = _iota(q.shape, 1)
    zero = jnp.zeros_like(q)
    q1 = jnp.where(lane < DIFF_DH, q, zero)
    q2 = jnp.where(lane >= DIFF_DH, q, zero)
    _init_softmax_state(acc_ref, m_ref, l_ref)
    r1, r2 = slice(0, tq), slice(tq, 2 * tq)

    def body(kc, vc):
        p1, a1 = _softmax_step(_dot_nt(q1, kc), m_ref, l_ref, r1)
        p2, a2 = _softmax_step(_dot_nt(q2, kc), m_ref, l_ref, r2)
        pv = _dot(jnp.concatenate([p1, p2], axis=0), vc)
        acc_ref[r1, :] = a1 * acc_ref[r1, :] + pv[:tq]
        acc_ref[r2, :] = a2 * acc_ref[r2, :] + pv[tq:]

    _key_spans(k_ref, v_ref, spans, body)
    lm = lam_ref[...]
    lam = (jnp.exp(jnp.sum(lm[0:1] * lm[1:2], axis=-1, keepdims=True))
           - jnp.exp(jnp.sum(lm[2:3] * lm[3:4], axis=-1, keepdims=True)) + lam_init)
    l1 = jnp.sum(l_ref[r1, :], axis=-1, keepdims=True)
    l2 = jnp.sum(l_ref[r2, :], axis=-1, keepdims=True)
    o = acc_ref[r1, :] / l1 - lam * (acc_ref[r2, :] / l2)
    y = o * lax.rsqrt(jnp.mean(o * o, axis=-1, keepdims=True) + RMS_EPS) * nw_ref[...]
    o_ref[0] = (y * (1.0 - lam_init)).astype(o_ref.dtype)


def _diff_attention(dq, dk, z, lam_p, norm_w, *, lam_init, tq, q_lo, nq, spans):
    B, T, _ = dq.shape
    return pl.pallas_call(
        functools.partial(_diff_kernel, tq=tq, spans=spans, lam_init=lam_init),
        grid=(B, N_HEADS, nq),
        in_specs=[pl.BlockSpec((1, tq, HEAD_DIM), lambda b, h, i: (b, q_lo + i, h)),
                  pl.BlockSpec((1, T, HEAD_DIM), lambda b, h, i: (b, 0, h)),
                  pl.BlockSpec((1, T, HEAD_DIM), lambda b, h, i: (b, 0, Z_DIFF_V // HEAD_DIM + h)),
                  pl.BlockSpec((8, LANES), lambda b, h, i: (0, 0)),
                  pl.BlockSpec((1, LANES), lambda b, h, i: (0, 0))],
        out_specs=pl.BlockSpec((1, tq, HEAD_DIM), lambda b, h, i: (b, i, h)),
        out_shape=jax.ShapeDtypeStruct((B, nq * tq, GROUP_WIDTH), _ACT),
        scratch_shapes=_attn_scratch(2 * tq),
        compiler_params=_cparams(("parallel", "parallel", "parallel")),
        name="diff_attention",
    )(dq, dk, z, lam_p, norm_w.reshape(1, LANES))


def _log_sigmoid(x):
    return jnp.minimum(x, 0.0) - jnp.log(1.0 + jnp.exp(-jnp.abs(x)))


def _softplus(x):
    return jnp.maximum(x, 0.0) + jnp.log(1.0 + jnp.exp(-jnp.abs(x)))


def _tri(n, kind, reps=1):
    assert n & (n - 1) == 0
    ii, jj = _iota((n, reps * n), 0), jnp.bitwise_and(_iota((n, reps * n), 1), n - 1)
    return {"le": ii >= jj, "lt": ii > jj, "ue": ii <= jj, "ut": ii < jj}[kind]


def _tri2(n):
    return jnp.concatenate([_tri(n, "le", 3), _tri(n, "ue", 3)], axis=0).astype(F32)


def _chain_orders(n_lat, n_ctx):
    fwd = [(n_lat, n_ctx, 1), (0, n_lat, 1)]
    rev = [(n_lat + n_ctx - 1, n_ctx, -1), (n_lat - 1, n_lat, -1)]
    return fwd, rev


def _head_post(o, nw, g):
    y = o * lax.rsqrt(jnp.mean(o * o, axis=-1, keepdims=True) + RMS_EPS) * nw
    return y * _silu(g)


def _gla_kernel(qk_ref, v_ref, g_ref, zm_ref, gw_ref, gb_ref, nw_ref, o_ref,
                u_ref, dl_ref, oi_ref, qg_ref, *, n_lat, n_ctx):
    C = CHUNK
    nch = n_lat + n_ctx
    le, ue = _tri(C, "le"), _tri(C, "ue")
    tri2 = _tri2(C)
    first = _iota((C, LANES), 1) < GLA_DK
    ones = jnp.ones((C, LANES), F32)
    gw, gb = gw_ref[0], gb_ref[0]
    qscale = GLA_DK ** -0.5

    def phase1(n, _):
        rows = pl.ds(pl.multiple_of(n * C, C), C)
        qk = qk_ref[0, rows, :].astype(F32)
        v = v_ref[0, rows, :]
        la = _log_sigmoid(_dot_hi(zm_ref[0, rows, :], gw) + gb) * (1.0 / GLA_TAU)
        pre, suf = _prefix_suffix(la, tri2)
        b = jnp.where(first, pre, suf)
        btot = jnp.where(first[0:1], b[C - 1:C, :], b[0:1, :])
        ref = b[C // 2:C // 2 + 1, :]
        qksw = pltpu.roll(qk, GLA_DK, 1)
        e1, e2 = jnp.exp(b - ref), jnp.exp(ref - b)
        zero = jnp.zeros_like(qk)
        qf = jnp.where(first, qk * e1, zero) * qscale
        kf = jnp.where(first, qksw * e2, zero)
        qr = jnp.where(first, zero, qksw * e1) * qscale
        kr = jnp.where(first, zero, qk * e2)
        a = jnp.where(le, _dot_nt(qf, kf), 0.0) + jnp.where(ue, _dot_nt(qr, kr), 0.0)
        oi_ref[rows, :] = _dot(a, v)
        qg_ref[rows, :] = jnp.where(first, qk, qksw) * jnp.exp(b) * qscale
        kd = jnp.where(first, qksw, qk) * jnp.exp(btot - b)
        u_ref[n] = _dot_tn(kd, v)
        la0, la1, la2 = _split(la, 3)
        dl_ref[n] = jnp.exp(_dot_tn(la0, ones) + (_dot_tn(la1, ones) + _dot_tn(la2, ones)))
        return 0

    lax.fori_loop(0, nch, phase1, 0, unroll=CHUNK_UNROLL)

    fwd, rev = _chain_orders(n_lat, n_ctx)
    hf, hr = slice(0, GLA_DK), slice(GLA_DK, 2 * GLA_DK)
    state = (jnp.zeros((GLA_DK, LANES), F32), jnp.zeros((GLA_DK, LANES), F32))
    for (f0, cnt, _), (r0, _, _) in zip(fwd, rev):
        def step(t, st, f0=f0, r0=r0):
            sf, sr = st
            cf, cr = f0 + t, r0 - t
            uf, df = u_ref[cf, hf, :], dl_ref[cf, hf, :]
            ur, dr = u_ref[cr, hr, :], dl_ref[cr, hr, :]
            u_ref[cf, hf, :] = sf
            u_ref[cr, hr, :] = sr
            return df * sf + uf, dr * sr + ur
        state = lax.fori_loop(0, cnt, step, state)

    def phase3(n, _):
        rows = pl.ds(pl.multiple_of(n * C, C), C)
        o = oi_ref[rows, :] + _dot(qg_ref[rows, :], u_ref[n])
        o_ref[0, rows, :] = _head_post(o, nw_ref[...], g_ref[0, rows, :].astype(F32)).astype(o_ref.dtype)
        return 0

    lax.fori_loop(0, nch, phase3, 0, unroll=CHUNK_UNROLL)


def _gla_mixer(z, zm, gw, gb, norm_w, n_lat_rows):
    B, T, _ = z.shape
    n_lat, n_ctx = n_lat_rows // CHUNK, (T - n_lat_rows) // CHUNK
    nch = n_lat + n_ctx
    zspec = lambda off: pl.BlockSpec((1, T, LANES), lambda b, h: (b, 0, off // LANES + h))
    return pl.pallas_call(
        functools.partial(_gla_kernel, n_lat=n_lat, n_ctx=n_ctx),
        grid=(B, N_HEADS),
        in_specs=[zspec(Z_GLA_QK), zspec(Z_GLA_V), zspec(Z_GLA_G),
                  pl.BlockSpec((1, T, LANES), lambda b, h: (b, 0, 0)),
                  pl.BlockSpec((1, LANES, LANES), lambda b, h: (h, 0, 0)),
                  pl.BlockSpec((1, 1, LANES), lambda b, h: (h, 0, 0)),
                  pl.BlockSpec((1, LANES), lambda b, h: (0, 0))],
        out_specs=pl.BlockSpec((1, T, LANES), lambda b, h: (b, 0, h)),
        out_shape=jax.ShapeDtypeStruct((B, T, GROUP_WIDTH), _ACT),
        scratch_shapes=[pltpu.VMEM((nch, LANES, LANES), F32), pltpu.VMEM((nch, LANES, LANES), F32),
                        pltpu.VMEM((T, LANES), F32), pltpu.VMEM((T, LANES), F32)],
        compiler_params=_cparams(("parallel", "parallel")),
        name="gla_mixer",
    )(z, z, z, zm, gw, gb, norm_w.reshape(1, LANES))


def _dn_gate_consts(C, h):
    trilu = jnp.concatenate([_tri(C, "le"), _tri(C, "ue")], axis=0).astype(_MXU)
    ri = _iota((8, 3 * LANES), 0)
    li = jnp.bitwise_and(_iota((8, 3 * LANES), 1), LANES - 1)
    sel3 = (((li == M_DNA + h) & (ri == 0)) | ((li == M_DNA + N_HEADS + h) & (ri == 1))).astype(_MXU)
    return _tri2(C).astype(_MXU), trilu, sel3


def _dn_gates(zm, gp, h, consts):
    C = zm.shape[0]
    lane = _iota(zm.shape, 1)
    first = lane < C
    tri2, trilu, sel3 = consts
    a_rows = _dot_nt(sel3, jnp.concatenate(_split(zm, 3), axis=1))
    pick = lambda off, d: jnp.sum(jnp.where(lane == off + d * N_HEADS + h, zm, 0.0), axis=-1, keepdims=True)
    g_cols, g_rows = [], []
    for d in range(2):
        na = -jnp.exp(gp[d:d + 1, :])
        dt = gp[2 + d:3 + d, :]
        g_cols.append(na * _softplus(pick(M_DNA, d) + dt))
        g_rows.append(na[:, :C] * _softplus(a_rows[d:d + 1, :] + dt[:, :C]))
    pre, suf = _prefix_suffix(jnp.where(first, g_cols[0], g_cols[1]), tri2)
    gc_cols = (jnp.where(first, pre, pltpu.roll(pre, C, 1)), jnp.where(first, pltpu.roll(suf, C, 1), suf))
    rsel = _iota((8, C), 0)
    stack = jnp.zeros((8, C), F32)
    for d in range(2):
        for i, piece in enumerate(_split(g_rows[d], 3)):
            stack = jnp.where(rsel == 3 * d + i, piece.astype(F32), stack)
    rc = _dot_nt(stack, trilu)
    gc_rows = (rc[0:1, :C] + rc[1:2, :C] + rc[2:3, :C],
               pltpu.roll(rc[3:4, :] + rc[4:5, :] + rc[5:6, :], C, 1)[:, :C])
    return [(jax.nn.sigmoid(pick(M_DNB, d)), gc_cols[d], gc_rows[d]) for d in range(2)]


def _dn_decay(gc_col, gc_row, mask):
    C = gc_row.shape[1]
    diff = jnp.where(mask, gc_col[:, :C] - gc_row, 0.0)
    return jnp.where(mask, jnp.exp(diff), 0.0)


def _dn_prep_kernel(qkv_ref, zm_ref, cw_ref, gp_ref, qkvn_ref, l_ref, xp_ref, *, n_lat_rows):
    C = CHUNK
    T = qkv_ref.shape[1]
    nch = T // C
    PAD = 8
    W = 3 * HEAD_DIM
    h = pl.program_id(1)
    xp_ref[0:PAD, :] = jnp.zeros((PAD, W), F32)
    xp_ref[PAD + T:PAD + T + PAD, :] = jnp.zeros((PAD, W), F32)

    def fill(n, _):
        rows = pl.ds(pl.multiple_of(n * C, C), C)
        xp_ref[pl.ds(pl.multiple_of(PAD + n * C, 8), C), :] = qkv_ref[0, rows, :].astype(F32)
        return 0

    lax.fori_loop(0, nch, fill, 0)
    cw = cw_ref[0]
    gp = gp_ref[0]
    lt, ut = _tri(C, "lt"), _tri(C, "ut")
    consts = _dn_gate_consts(C, h)
    tcol = _iota((C, 1), 0)
    half = SHORT_CONV // 2

    def chunk(n, _):
        r0 = pl.multiple_of(n * C, C)
        rows = pl.ds(r0, C)
        win = xp_ref[pl.ds(r0, C + 2 * PAD), :]
        seg = r0 >= n_lat_rows
        acc = jnp.zeros((C, W), F32)
        for i in range(SHORT_CONV):
            xs = win[PAD + i - half:PAD + i - half + C, :]
            if i != half:
                same = ((r0 + tcol + (i - half)) >= n_lat_rows) == seg
                xs = jnp.where(same, xs, 0.0)
            acc = acc + cw[i:i + 1, :] * xs
        y = _silu(acc)
        q, k, v = y[:, :HEAD_DIM], y[:, HEAD_DIM:2 * HEAD_DIM], y[:, 2 * HEAD_DIM:]
        qn = q * lax.rsqrt(jnp.sum(q * q, axis=-1, keepdims=True) + RMS_EPS) * HEAD_DIM ** -0.5
        kn = k * lax.rsqrt(jnp.sum(k * k, axis=-1, keepdims=True) + RMS_EPS)
        qkvn_ref[0, 0, rows, :] = jnp.concatenate([qn, kn, v], axis=1).astype(qkvn_ref.dtype)
        kk = _dot_nt(kn, kn)
        (bf, gcf, grf), (br, gcr, grr) = _dn_gates(zm_ref[0, rows, :], gp, h, consts)
        l_ref[0, 0, 0, n] = kk * bf * _dn_decay(gcf, grf, lt)
        l_ref[1, 0, 0, n] = kk * br * _dn_decay(gcr, grr, ut)
        return 0

    lax.fori_loop(0, nch, chunk, 0, unroll=CHUNK_UNROLL)


def _dn_prep(z, zm, conv_w, gp, n_lat_rows):
    B, T, _ = z.shape
    nch = T // CHUNK
    W = 3 * HEAD_DIM
    return pl.pallas_call(
        functools.partial(_dn_prep_kernel, n_lat_rows=n_lat_rows),
        grid=(B, N_HEADS),
        in_specs=[pl.BlockSpec((1, T, W), lambda b, h: (b, 0, Z_DN_QKV // W + h)),
                  pl.BlockSpec((1, T, LANES), lambda b, h: (b, 0, 0)),
                  pl.BlockSpec((1, 8, W), lambda b, h: (h, 0, 0)),
                  pl.BlockSpec((1, 8, LANES), lambda b, h: (h, 0, 0))],
        out_specs=[pl.BlockSpec((1, 1, T, W), lambda b, h: (b, h, 0, 0)),
                   pl.BlockSpec((2, 1, 1, nch, CHUNK, CHUNK), lambda b, h: (0, b, h, 0, 0, 0))],
        out_shape=[jax.ShapeDtypeStruct((B, N_HEADS, T, W), _ACT),
                   jax.ShapeDtypeStruct((2, B, N_HEADS, nch, CHUNK, CHUNK), F32)],
        scratch_shapes=[pltpu.VMEM((T + 16, W), F32)],
        compiler_params=_cparams(("parallel", "parallel")),
        name="dn_prep",
    )(z, zm, conv_w, gp)


def _tri_solve_kernel(l_ref, t_ref, *, n_lower):
    C = CHUNK
    t_ref[...] = jnp.zeros_like(t_ref)
    cidx = _iota((C, LANES), 0)

    def solve_row(i, jb_lo, jb_hi):
        def blk(jb, acc):
            j0 = pl.multiple_of(jb * 8, 8)
            l8 = l_ref[0, i, pl.ds(j0, 8), :]
            for k in range(8):
                acc = acc - l8[k:k + 1, :] * t_ref[0, j0 + k]
            return acc
        t_ref[0, i] = lax.fori_loop(jb_lo, jb_hi, blk, (cidx == i).astype(F32))

    @pl.when(pl.program_id(0) < n_lower)
    def _():
        def row(i, _):
            solve_row(i, 0, (i + 7) // 8)
            return 0
        lax.fori_loop(0, C, row, 0)

    @pl.when(pl.program_id(0) >= n_lower)
    def _():
        def row(t, _):
            i = C - 1 - t
            solve_row(i, i // 8, C // 8)
            return 0
        lax.fori_loop(0, C, row, 0)


def _tri_solve(lt, n_lower):
    G = lt.shape[0]
    spec = pl.BlockSpec((1, CHUNK, CHUNK, LANES), lambda g: (g, 0, 0, 0))
    return pl.pallas_call(
        functools.partial(_tri_solve_kernel, n_lower=n_lower), grid=(G,), in_specs=[spec], out_specs=spec,
        out_shape=jax.ShapeDtypeStruct(lt.shape, F32),
        compiler_params=_cparams(("parallel",)),
        name="dn_tri_solve",
    )(lt)


def _dn_scan_kernel(qkvn_ref, t_ref, zm_ref, gp_ref, g_ref, nw_ref, o_ref,
                    mk_ref, ns_ref, dl_ref, u_ref, w_ref, qg_ref, at_ref, *, n_lat, n_ctx):
    C = CHUNK
    nch = n_lat + n_ctx
    h = pl.program_id(1)
    gp = gp_ref[0]
    masks = (_tri(C, "le"), _tri(C, "ue"))
    consts = _dn_gate_consts(C, h)
    last_row = (C - 1, 0)

    def phase1(n, _):
        rows = pl.ds(pl.multiple_of(n * C, C), C)
        x = qkvn_ref[0, 0, rows, :].astype(F32)
        qn, kn, vs = x[:, :HEAD_DIM], x[:, HEAD_DIM:2 * HEAD_DIM], x[:, 2 * HEAD_DIM:]
        qk = _dot_nt(qn, kn)
        gates = _dn_gates(zm_ref[0, rows, :], gp, h, consts)
        for d in range(2):
            beta, gc, gr = gates[d]
            egc = jnp.exp(gc)
            uw = _dot(t_ref[d, 0, 0, n], jnp.concatenate([vs * beta, kn * beta * egc], axis=1))
            u, w = uw[:, :HEAD_DIM], uw[:, HEAD_DIM:]
            glast = gc[last_row[d]:last_row[d] + 1, :]
            kd = kn * jnp.exp(glast - gc)
            mn = _dot_tn(kd, uw)
            mk_ref[d, n] = (-mn[:, HEAD_DIM:]).astype(mk_ref.dtype)
            ns_ref[d, n] = mn[:, :HEAD_DIM]
            dl_ref[d, n] = jnp.broadcast_to(jnp.exp(glast), (8, LANES))
            u_ref[d, rows, :] = u
            w_ref[d, rows, :] = w.astype(w_ref.dtype)
            qg_ref[d, rows, :] = (qn * egc).astype(qg_ref.dtype)
            at_ref[d, n] = (qk * _dn_decay(gc, gr, masks[d])).astype(at_ref.dtype)
        return 0

    lax.fori_loop(0, nch, phase1, 0, unroll=CHUNK_UNROLL)

    fwd, rev = _chain_orders(n_lat, n_ctx)
    state = (jnp.zeros((HEAD_DIM, HEAD_DIM), F32), jnp.zeros((HEAD_DIM, HEAD_DIM), F32))
    for (f0, cnt, _), (r0, _, _) in zip(fwd, rev):
        def step(t, st, f0=f0, r0=r0):
            new = []
            for d, c in ((0, f0 + t), (1, r0 - t)):
                s = st[d]
                nn = ns_ref[d, c]
                ns_ref[d, c] = s
                new.append(dl_ref[d, c, 0:1, :] * s + _dot(mk_ref[d, c], s) + nn)
            return tuple(new)
        state = lax.fori_loop(0, cnt, step, state)

    def phase3(n, _):
        rows = pl.ds(pl.multiple_of(n * C, C), C)
        o = jnp.zeros((C, HEAD_DIM), F32)
        for d in range(2):
            s = ns_ref[d, n]
            ws = _dot(jnp.concatenate([w_ref[d, rows, :], qg_ref[d, rows, :]], axis=0), s)
            v_new = u_ref[d, rows, :] - ws[:C]
            o = o + ws[C:] + _dot(at_ref[d, n], v_new)
        o_ref[0, rows, :] = _head_post(o, nw_ref[...], g_ref[0, rows, :].astype(F32)).astype(o_ref.dtype)
        return 0

    lax.fori_loop(0, nch, phase3, 0, unroll=CHUNK_UNROLL)


def _dn_scan(qkvn, tmat, z, zm, gp, norm_w, n_lat_rows):
    B, T, _ = z.shape
    n_lat, n_ctx = n_lat_rows // CHUNK, (T - n_lat_rows) // CHUNK
    nch = n_lat + n_ctx
    W = 3 * HEAD_DIM
    return pl.pallas_call(
        functools.partial(_dn_scan_kernel, n_lat=n_lat, n_ctx=n_ctx),
        grid=(B, N_HEADS),
        in_specs=[pl.BlockSpec((1, 1, T, W), lambda b, h: (b, h, 0, 0)),
                  pl.BlockSpec((2, 1, 1, nch, CHUNK, CHUNK), lambda b, h: (0, b, h, 0, 0, 0)),
                  pl.BlockSpec((1, T, LANES), lambda b, h: (b, 0, 0)),
                  pl.BlockSpec((1, 8, LANES), lambda b, h: (h, 0, 0)),
                  pl.BlockSpec((1, T, LANES), lambda b, h: (b, 0, Z_DN_G // LANES + h)),
                  pl.BlockSpec((1, LANES), lambda b, h: (0, 0))],
        out_specs=pl.BlockSpec((1, T, LANES), lambda b, h: (b, 0, h)),
        out_shape=jax.ShapeDtypeStruct((B, T, GROUP_WIDTH), _ACT),
        scratch_shapes=[pltpu.VMEM((2, nch, HEAD_DIM, HEAD_DIM), _MXU),
                        pltpu.VMEM((2, nch, HEAD_DIM, HEAD_DIM), F32),
                        pltpu.VMEM((2, nch, 8, LANES), F32),
                        pltpu.VMEM((2, T, HEAD_DIM), F32),
                        pltpu.VMEM((2, T, HEAD_DIM), _MXU),
                        pltpu.VMEM((2, T, HEAD_DIM), _MXU),
                        pltpu.VMEM((2, nch, CHUNK, CHUNK), _MXU)],
        compiler_params=_cparams(("parallel", "parallel")),
        name="dn_scan",
    )(qkvn, tmat, zm, gp, z, norm_w.reshape(1, LANES))


def _dn_mixer(z, zm, conv_w, gp, norm_w, n_lat_rows):
    B, T, _ = z.shape
    qkvn, lmat = _dn_prep(z, zm, conv_w, gp, n_lat_rows)
    n_sys = B * N_HEADS * (T // CHUNK)
    n_grp = pl.cdiv(n_sys, LANES)
    flat = lmat.reshape(2, n_sys, CHUNK, CHUNK)
    flat = jnp.pad(flat, ((0, 0), (0, n_grp * LANES - n_sys), (0, 0), (0, 0)))
    sol = _tri_solve(flat.reshape(2 * n_grp, LANES, CHUNK, CHUNK).transpose(0, 2, 3, 1), n_grp)
    sol = sol.transpose(0, 3, 1, 2).reshape(2, n_grp * LANES, CHUNK, CHUNK)[:, :n_sys]
    tmat = sol.reshape(lmat.shape)
    return _dn_scan(qkvn, tmat, z, zm, gp, norm_w, n_lat_rows)


def _out_proj_kernel(*refs, n_lat_tiles, with_ctx):
    n_in = 4 + (2 if with_ctx else 0)
    parts = list(refs[:4])
    ctx_parts = refs[4:n_in]
    w_ref, x_ref, mt_ref, nw_ref, rw_ref, xo_ref, at_ref = refs[n_in:]
    D = x_ref.shape[2]
    is_ctx = pl.program_id(1) >= n_lat_tiles
    o = None
    for g, p_ref in enumerate(parts):
        p = p_ref[0]
        if with_ctx and g >= 2:
            p = jnp.where(is_ctx, ctx_parts[g - 2][0], p)
        t = jnp.dot(p, w_ref[g * GROUP_WIDTH:(g + 1) * GROUP_WIDTH, :], preferred_element_type=F32)
        o = t if o is None else o + t
    xn = x_ref[0] + mt_ref[0, 0, 2:3, :] * o
    xo_ref[0, :, 0:D] = xn
    y = xn * lax.rsqrt(jnp.mean(xn * xn, axis=-1, keepdims=True) + RMS_EPS) * nw_ref[...]
    h2 = y * (1.0 + mt_ref[0, 0, 4:5, :]) + mt_ref[0, 0, 3:4, :]
    logits = _dot_hi(h2, rw_ref[...])
    valid = _iota(logits.shape, 1) < N_EXPERTS
    lg = jnp.where(valid, logits, -jnp.inf)
    e = jnp.exp(lg - jnp.max(lg, axis=-1, keepdims=True))
    aff = e / jnp.sum(e, axis=-1, keepdims=True)
    xo_ref[0, :, D:2 * D] = h2
    xo_ref[0, :, 2 * D:] = aff
    at_ref[0] = aff.T[:N_EXPERTS, :]


def _out_proj(parts, ctx_parts, w_out, tok, mtab, norm_w, router_w, n_lat_tiles):
    B, T, _ = tok.shape
    D = w_out.shape[0]
    with_ctx = ctx_parts is not None
    n_tiles = T // ROW_TILE if with_ctx else n_lat_tiles
    rows = n_tiles * ROW_TILE
    full = pl.BlockSpec((1, ROW_TILE, GROUP_WIDTH), lambda b, m: (b, m, 0))
    lat = pl.BlockSpec((1, ROW_TILE, GROUP_WIDTH), lambda b, m: (b, jnp.minimum(m, n_lat_tiles - 1), 0))
    ctx = pl.BlockSpec((1, ROW_TILE, GROUP_WIDTH), lambda b, m: (b, jnp.maximum(m - n_lat_tiles, 0), 0))
    xspec = pl.BlockSpec((1, ROW_TILE, D), lambda b, m: (b, m, 0))
    in_specs = [full, full, lat, lat] + ([ctx, ctx] if with_ctx else [])
    args = list(parts) + (list(ctx_parts) if with_ctx else [])
    return pl.pallas_call(
        functools.partial(_out_proj_kernel, n_lat_tiles=n_lat_tiles, with_ctx=with_ctx),
        grid=(B, n_tiles),
        in_specs=in_specs + [pl.BlockSpec((D, D), lambda b, m: (0, 0)),
                             xspec,
                             pl.BlockSpec((1, 1, 8, D), lambda b, m: (b, m // n_lat_tiles, 0, 0)),
                             pl.BlockSpec((1, D), lambda b, m: (0, 0)),
                             pl.BlockSpec((D, LANES), lambda b, m: (0, 0))],
        out_specs=[pl.BlockSpec((1, ROW_TILE, 2 * D + LANES), lambda b, m: (b, m, 0)),
                   pl.BlockSpec((1, N_EXPERTS, ROW_TILE), lambda b, m: (b, 0, m))],
        out_shape=[jax.ShapeDtypeStruct((B, rows, 2 * D + LANES), F32),
                   jax.ShapeDtypeStruct((B, N_EXPERTS, rows), F32)],
        compiler_params=_cparams(("parallel", "parallel")),
        name="out_proj",
    )(*args, w_out, tok, mtab, norm_w.reshape(1, D), router_w)


def _route_kernel(at_ref, idx_ref, rank_ref, *, lo, n, cap):
    E = N_EXPERTS
    aff = at_ref[0, :, lo:lo + n]
    tok = _iota((E, n), 1)
    count = lambda m: jnp.sum(m.astype(F32), axis=-1, keepdims=True)
    capf = float(cap)

    def vbit(i, t):
        cand = t | jnp.left_shift(jnp.int32(1), 30 - i)
        return jnp.where(count(aff >= pltpu.bitcast(cand, F32)) >= capf, cand, t)

    thr = pltpu.bitcast(lax.fori_loop(0, 31, vbit, jnp.zeros((E, 1), jnp.int32)), F32)
    gt, eq = aff > thr, aff == thr
    need = capf - count(gt)

    def ibit(i, m):
        cand = m | jnp.left_shift(jnp.int32(1), 12 - i)
        return jnp.where(count(eq & (tok < cand)) < need, cand, m)

    m = lax.fori_loop(0, 13, ibit, jnp.zeros((E, 1), jnp.int32))
    sel = gt | (eq & (tok <= m))
    ustrict = _tri(LANES, "ut").astype(F32)
    carry = jnp.zeros((E, 1), F32)
    for t in range(n // LANES):
        sl = slice(t * LANES, (t + 1) * LANES)
        s = sel[:, sl].astype(F32)
        rank_ref[:, sl] = jnp.where(sel[:, sl], _dot(s, ustrict) + carry, -1.0)
        carry = carry + jnp.sum(s, axis=-1, keepdims=True)
    tokc = _iota((8, n), 1)
    rsel = _iota((8, n), 0)
    rn = jnp.where(rsel == 0, tokc // 64, jnp.where(rsel == 1, tokc % 64, 0)).astype(F32)

    def per_expert(e, _):
        rk = rank_ref[pl.ds(e, 1), :]
        onehot = (rk == _iota((cap, n), 0).astype(F32)).astype(F32)
        res = _dot_nt(rn, onehot)
        idx_ref[0, pl.ds(e, 1), :] = (res[0:1, :] * 64.0 + res[1:2, :]).astype(jnp.int32)
        return 0

    lax.fori_loop(0, E, per_expert, 0)


def _route(aff_t, lo, n, cap):
    B, E, T = aff_t.shape
    return pl.pallas_call(
        functools.partial(_route_kernel, lo=lo, n=n, cap=cap),
        grid=(B,),
        in_specs=[pl.BlockSpec((1, E, T), lambda b: (b, 0, 0))],
        out_specs=pl.BlockSpec((1, E, cap), lambda b: (b, 0, 0)),
        out_shape=jax.ShapeDtypeStruct((B, E, cap), jnp.int32),
        scratch_shapes=[pltpu.VMEM((E, n), F32)],
        compiler_params=_cparams(("parallel",)),
        name="route_topk",
    )(aff_t)


def _moe_kernel(idx_ref, comb_hbm, gt_ref, wg_ref, wu_ref, wd_ref, out_hbm, *scratch, groups):
    del comb_hbm
    e = pl.program_id(0)
    D = gt_ref.shape[1]
    bufs, sem = scratch[:MOE_BUFS], scratch[MOE_BUFS]
    sem_g = lambda g: sem.at[g % MOE_BUFS]
    sem_s = lambda g: sem.at[MOE_BUFS + g % MOE_BUFS]

    def start_gather(g):
        base, n, _ = groups[g]
        for r in range(n):
            row = pl.ds(idx_ref[e, base + r], 1)
            pltpu.make_async_copy(out_hbm.at[row, :], bufs[g % MOE_BUFS].at[pl.ds(r, 1), :], sem_g(g)).start()

    def start_scatter(g):
        base, n, _ = groups[g]
        for r in range(n):
            row = pl.ds(idx_ref[e, base + r], 1)
            pltpu.make_async_copy(bufs[g % MOE_BUFS].at[pl.ds(r, 1), pl.ds(0, D)], out_hbm.at[row, pl.ds(0, D)],
                                  sem_s(g)).start()

    def wait_gather(g):
        rows = pl.ds(0, groups[g][1])
        pltpu.make_async_copy(out_hbm.at[rows, :], bufs[g % MOE_BUFS].at[rows, :], sem_g(g)).wait()

    def wait_scatter(g):
        rows = pl.ds(0, groups[g][1])
        pltpu.make_async_copy(bufs[g % MOE_BUFS].at[rows, pl.ds(0, D)], out_hbm.at[rows, pl.ds(0, D)], sem_s(g)).wait()

    def compute(g):
        _, n, grow = groups[g]
        buf = bufs[g % MOE_BUFS]
        xs = buf[0:n, D:2 * D].astype(_MXU)
        aff = buf[0:n, 2 * D:]
        wt = jnp.sum(jnp.where(_iota(aff.shape, 1) == e, aff, 0.0), axis=-1, keepdims=True)
        hid = _silu(jnp.dot(xs, wg_ref[0], preferred_element_type=F32)) * \
            jnp.dot(xs, wu_ref[0], preferred_element_type=F32)
        y = jnp.dot(hid.astype(_MXU), wd_ref[0], preferred_element_type=F32) * wt
        buf[0:n, 0:D] = buf[0:n, 0:D] + gt_ref[grow:grow + 1, :] * y

    G = len(groups)
    for g in range(min(2, G)):
        start_gather(g)
    for g in range(G):
        wait_gather(g)
        if g >= 2:
            wait_scatter(g - 2)
        if g + 2 < G:
            start_gather(g + 2)
        if g >= 1:
            start_scatter(g - 1)
        compute(g)
    start_scatter(G - 1)
    for g in range(max(G - 2, 0), G):
        wait_scatter(g)


MOE_ROWS = 256
MOE_BUFS = 4


def _moe_groups(B, cap_lat, cap_ctx):
    groups = []
    for b in range(B):
        for off in range(0, cap_lat, MOE_ROWS):
            groups.append((b * cap_lat + off, min(MOE_ROWS, cap_lat - off), b))
    if cap_ctx:
        assert B * cap_ctx <= MOE_ROWS
        groups.append((B * cap_lat, B * cap_ctx, B))
    return tuple(groups)


def _moe(idx, comb, gtab, wg, wu, wd, groups):
    E, R = idx.shape
    BT, W = comb.shape
    D, FF = wg.shape[1], wg.shape[2]
    grid_spec = pltpu.PrefetchScalarGridSpec(
        num_scalar_prefetch=1,
        grid=(E,),
        in_specs=[pl.BlockSpec(memory_space=pl.ANY),
                  pl.BlockSpec((8, D), lambda e, idx: (0, 0)),
                  pl.BlockSpec((1, D, FF), lambda e, idx: (e, 0, 0)),
                  pl.BlockSpec((1, D, FF), lambda e, idx: (e, 0, 0)),
                  pl.BlockSpec((1, FF, D), lambda e, idx: (e, 0, 0))],
        out_specs=pl.BlockSpec(memory_space=pl.ANY),
        scratch_shapes=[pltpu.VMEM((MOE_ROWS, W), F32)] * MOE_BUFS + [pltpu.SemaphoreType.DMA((2 * MOE_BUFS,))],
    )
    return pl.pallas_call(
        functools.partial(_moe_kernel, groups=groups),
        grid_spec=grid_spec,
        out_shape=jax.ShapeDtypeStruct((BT, W), F32),
        input_output_aliases={1: 0},
        compiler_params=_cparams(("arbitrary",)),
        name="moe_experts",
    )(idx, comb, gtab, wg, wu, wd)


def _final_norm_kernel(x_ref, w_ref, o_ref):
    x = x_ref[0]
    o_ref[0] = x * lax.rsqrt(jnp.mean(x * x, axis=-1, keepdims=True) + RMS_EPS) * w_ref[...]


def _final_norm(tok, w, n_lat_rows):
    B = tok.shape[0]
    D = w.shape[0]
    spec = pl.BlockSpec((1, ROW_TILE, D), lambda b, m: (b, m, 0))
    return pl.pallas_call(
        _final_norm_kernel,
        grid=(B, n_lat_rows // ROW_TILE),
        in_specs=[spec, pl.BlockSpec((1, D), lambda b, m: (0, 0))],
        out_specs=spec,
        out_shape=jax.ShapeDtypeStruct((B, n_lat_rows, D), F32),
        compiler_params=_cparams(("parallel", "parallel")),
        name="final_norm",
    )(tok, w.reshape(1, D))


def _in_proj_columns():
    o = {}
    off = 0
    for name, w in (("gla_q", 256), ("gla_k", 256), ("gla_v", 512), ("gla_g", 512), ("gla_lr", 32),
                    ("dn_qkv", 1536), ("dn_g", 512), ("dn_a", 8), ("dn_b", 8), ("gqa_q", 512),
                    ("gqa_kv", 512), ("diff_q", 512), ("diff_k", 512), ("diff_v", 512)):
        o[name] = off
        off += w
    rng = lambda a, n: list(range(a, a + n))
    cols = []
    for h in range(N_HEADS):
        cols += rng(o["gla_q"] + GLA_DK * h, GLA_DK) + rng(o["gla_k"] + GLA_DK * h, GLA_DK)
    cols += rng(o["gla_v"], 512) + rng(o["gla_g"], 512)
    for h in range(N_HEADS):
        for part in range(3):
            cols += rng(o["dn_qkv"] + part * GROUP_WIDTH + HEAD_DIM * h, HEAD_DIM)
    cols += rng(o["dn_g"], 512) + rng(o["gqa_q"], 512) + rng(o["gqa_kv"], 512)
    cols += rng(o["diff_q"], 512) + rng(o["diff_k"], 512) + rng(o["diff_v"], 512)
    assert len(cols) == Z_WIDTH
    misc = rng(o["gla_lr"], 32) + rng(o["dn_a"], 8) + rng(o["dn_b"], 8)
    dn_cols = [c - o["dn_qkv"] for c in cols[Z_DN_QKV:Z_DN_QKV + 3 * GROUP_WIDTH]]
    return np.asarray(cols), np.asarray(misc), np.asarray(dn_cols)


def _runs(cols):
    cuts = [0] + [i for i in range(1, len(cols)) if cols[i] != cols[i - 1] + 1] + [len(cols)]
    return [(int(cols[a]), int(cols[b - 1]) + 1) for a, b in zip(cuts[:-1], cuts[1:])]


def _layer_params(l, p):
    cols, misc, dn_cols = _in_proj_columns()
    D = p["w_in"].shape[1]
    w_in = p["w_in"][l]
    w_big = jnp.concatenate([w_in[:, a:b] for a, b in _runs(cols)], axis=1).astype(_MXU)
    w_misc = jnp.pad(w_in[:, misc], ((0, 0), (0, LANES - len(misc)))).astype(_MXU)
    w2, gb2 = p["gla_gate_w2"][l], p["gla_gate_b"][l]
    gw = jnp.zeros((N_HEADS, LANES, LANES), F32)
    gb = []
    for h in range(N_HEADS):
        hs = slice(h * GLA_DK, (h + 1) * GLA_DK)
        gw = gw.at[h, 0:GLA_RANK, 0:GLA_DK].set(w2[0][:, hs])
        gw = gw.at[h, GLA_RANK:2 * GLA_RANK, GLA_DK:2 * GLA_DK].set(w2[1][:, hs])
        gb.append(jnp.concatenate([gb2[0][hs], gb2[1][hs]])[None, :])
    gb = jnp.stack(gb)
    conv = p["dn_conv_w"][l][:, dn_cols].reshape(SHORT_CONV, N_HEADS, 3 * HEAD_DIM).transpose(1, 0, 2)
    conv = jnp.pad(conv, ((0, 0), (0, 8 - SHORT_CONV), (0, 0)))
    al, dt = p["dn_a_log"][l], p["dn_dt_bias"][l]
    gp = jnp.stack([al[0], al[1], dt[0], dt[1]], axis=0).T
    gp = jnp.pad(jnp.broadcast_to(gp[:, :, None], (N_HEADS, 4, LANES)), ((0, 0), (0, 4), (0, 0)))
    lam_p = jnp.pad(p["diff_lambda"][l], ((0, 4), (0, LANES - DIFF_DH)))
    router = jnp.pad(p["router_w"][l], ((0, 0), (0, LANES - N_EXPERTS)))
    return dict(w_big=w_big, w_misc=w_misc, gw=gw, gb=gb, conv=conv, gp=gp, lam_p=lam_p, router=router,
                w_out=p["w_out"][l].astype(_MXU),
                wg=p["exp_w_gate"][l].astype(_MXU), wu=p["exp_w_up"][l].astype(_MXU),
                wd=p["exp_w_down"][l].astype(_MXU))


def _mixers(z, zm, lp, p, l, tabs, n_lat_rows, need_ctx, lam_init):
    T = z.shape[1]
    gla = _gla_mixer(z, zm, lp["gw"], lp["gb"], p["gla_norm_w"][l], n_lat_rows)
    dn = _dn_mixer(z, zm, lp["conv"], lp["gp"], p["dn_norm_w"][l], n_lat_rows)
    gq, gk, dq, dk = _attn_prep(z, p["gqa_q_norm"][l], p["gqa_k_norm"][l], tabs)
    tq = 2 * ROW_TILE
    ctx_span = (n_lat_rows, ROW_TILE, (T - n_lat_rows) // ROW_TILE)
    lat = dict(tq=tq, q_lo=0, nq=n_lat_rows // tq, spans=((0, tq, n_lat_rows // tq), ctx_span))
    ctx = dict(tq=ROW_TILE, q_lo=n_lat_rows // ROW_TILE, nq=(T - n_lat_rows) // ROW_TILE, spans=(ctx_span,))
    diff_args = (dq, dk, z, lp["lam_p"], p["diff_norm_w"][l])
    parts = (gla, dn, _gqa_attention(gq, gk, z, **lat), _diff_attention(*diff_args, lam_init=lam_init, **lat))
    ctx_parts = None
    if need_ctx:
        ctx_parts = (_gqa_attention(gq, gk, z, **ctx), _diff_attention(*diff_args, lam_init=lam_init, **ctx))
    return parts, ctx_parts


def _forward(x, c, ctx, c_ctx, p):
    B, N, D = x.shape
    n_ctx = ctx.shape[1]
    T = N + n_ctx
    L = p["mod_w"].shape[0]
    n_lat_tiles = N // ROW_TILE
    cc = jnp.concatenate([c, c_ctx[None, :], jnp.zeros((8 - B - 1, D), F32)], axis=0)
    mods = _modulation(cc, p["mod_w"], p["mod_b"])
    tok = jnp.concatenate([x, ctx], axis=1)
    tabs = _rope_tables(N)
    cap_lat = EC_CAPACITY * N // N_EXPERTS
    cap_ctx = EC_CAPACITY * n_ctx // N_EXPERTS
    for l in range(L):
        last = l == L - 1
        lam_init = 0.8 - 0.6 * math.exp(-0.3 * l)
        lp = _layer_params(l, p)
        m6 = mods[l].reshape(8, 6, D)
        mtab = jnp.stack([m6[:B], jnp.broadcast_to(m6[B:B + 1], (B, 6, D))], axis=1)
        mtab = jnp.pad(mtab, ((0, 0), (0, 0), (0, 2), (0, 0)))
        z, zm = _in_proj(tok, mtab, p["norm1_w"][l], lp["w_big"], lp["w_misc"], n_lat_tiles)
        parts, ctx_parts = _mixers(z, zm, lp, p, l, tabs, N, not last, lam_init)
        tok, aff_t = _out_proj(parts, ctx_parts, lp["w_out"], tok, mtab, p["norm2_w"][l], lp["router"],
                               n_lat_tiles)
        rows_out, W = tok.shape[1:]
        base = (jnp.arange(B, dtype=jnp.int32) * rows_out)[None, :, None]
        idx = (_route(aff_t, 0, N, cap_lat).transpose(1, 0, 2) + base).reshape(N_EXPERTS, B * cap_lat)
        if not last:
            ic = _route(aff_t, N, n_ctx, cap_ctx).transpose(1, 0, 2) + base + N
            idx = jnp.concatenate([idx, ic.reshape(N_EXPERTS, B * cap_ctx)], axis=1)
        groups = _moe_groups(B, cap_lat, 0 if last else cap_ctx)
        gtab = jnp.pad(m6[:B + 1, 5, :], ((0, 8 - B - 1), (0, 0)))
        tok = _moe(idx, tok.reshape(B * rows_out, W), gtab, lp["wg"], lp["wu"], lp["wd"],
                   groups).reshape(B, rows_out, W)
    return _final_norm(tok, p["final_norm_w"], N)


def kernel(x, c, ctx, c_ctx, mod_w, mod_b, norm1_w, norm2_w, w_in, w_out, gla_gate_w2, gla_gate_b, gla_norm_w,
           dn_conv_w, dn_a_log, dn_dt_bias, dn_norm_w, gqa_q_norm, gqa_k_norm, diff_lambda, diff_norm_w,
           router_w, exp_w_gate, exp_w_up, exp_w_down, final_norm_w):
    p = dict(mod_w=mod_w, mod_b=mod_b, norm1_w=norm1_w, norm2_w=norm2_w, w_in=w_in, w_out=w_out,
             gla_gate_w2=gla_gate_w2, gla_gate_b=gla_gate_b, gla_norm_w=gla_norm_w, dn_conv_w=dn_conv_w,
             dn_a_log=dn_a_log, dn_dt_bias=dn_dt_bias, dn_norm_w=dn_norm_w, gqa_q_norm=gqa_q_norm,
             gqa_k_norm=gqa_k_norm, diff_lambda=diff_lambda, diff_norm_w=diff_norm_w, router_w=router_w,
             exp_w_gate=exp_w_gate, exp_w_up=exp_w_up, exp_w_down=exp_w_down, final_norm_w=final_norm_w)
    return _forward(x, c, ctx, c_ctx, p)
```

```python
import functools
import math

import jax
import jax.numpy as jnp
import numpy as np
from jax import lax
from jax.experimental import pallas as pl
from jax.experimental.pallas import tpu as pltpu

F32 = jnp.float32
_MXU = jnp.bfloat16
_ACT = jnp.bfloat16

D_MODEL = 2048
N_CTX = 256
GRID_W = 64
HEAD_DIM = 128
N_HEADS = 4
GROUP_WIDTH = N_HEADS * HEAD_DIM
GLA_DK = 64
GLA_RANK = 16
GLA_TAU = 16.0
SHORT_CONV = 5
GQA_KV_HEADS = 2
DIFF_DH = 64
ROPE_THETA = 10000.0
N_EXPERTS = 16
EC_CAPACITY = 2
EXPERT_FF = D_MODEL // 2
RMS_EPS = 1e-6

LANES = 128
ROW_TILE = 256
CHUNK = 64
CHUNK_UNROLL = 4
VMEM_LIMIT = 56 * 1024 * 1024

Z_GLA_QK = 0
Z_GLA_V = 512
Z_GLA_G = 1024
Z_DN_QKV = 1536
Z_DN_G = 3072
Z_GQA_Q = 3584
Z_GQA_KV = 4096
Z_DIFF_Q = 4608
Z_DIFF_K = 5120
Z_DIFF_V = 5632
Z_WIDTH = 6144
M_LR = 0
M_DNA = 32
M_DNB = 40


def _cparams(sem, vmem=VMEM_LIMIT):
    return pltpu.CompilerParams(dimension_semantics=sem, vmem_limit_bytes=vmem)


def _dot(a, b):
    return jnp.dot(a.astype(_MXU), b.astype(_MXU), preferred_element_type=F32)


def _dot_nt(a, b):
    return lax.dot_general(a.astype(_MXU), b.astype(_MXU), (((1,), (1,)), ((), ())),
                           preferred_element_type=F32)


def _dot_tn(a, b):
    return lax.dot_general(a.astype(_MXU), b.astype(_MXU), (((0,), (0,)), ((), ())),
                           preferred_element_type=F32)


def _split(a, n):
    parts = []
    r = a
    for _ in range(n - 1):
        p = r.astype(_MXU)
        parts.append(p)
        r = r - p.astype(F32)
    parts.append(r.astype(_MXU))
    return parts


def _dot_hi(a, b):
    a0, a1 = _split(a, 2)
    b0, b1 = _split(b, 2)
    return _dot(a0, b0) + (_dot(a1, b0) + _dot(a0, b1))


def _hi_lhs(a):
    a0, a1 = _split(a, 2)
    return jnp.concatenate([a0, a1, a0], axis=1)


def _hi_rhs(b):
    b0, b1 = _split(b, 2)
    return jnp.concatenate([b0, b0, b1], axis=0)


def _split_rows(a, n=3):
    return jnp.concatenate(_split(a, n), axis=0)


def _prefix_suffix(x, tri2):
    C = x.shape[0]
    out = _dot(tri2, _split_rows(x))
    return out[:C], out[C:]


def _silu(x):
    return x * jax.nn.sigmoid(x)


def _iota(shape, dim):
    return lax.broadcasted_iota(jnp.int32, shape, dim)


def _mod_kernel(cc_ref, w_ref, b_ref, o_ref):
    a = _silu(cc_ref[...])
    o_ref[0] = _dot_hi(a, w_ref[0]) + b_ref[0]


def _modulation(cc, mod_w, mod_b):
    L, D, W = mod_w.shape
    tn = 1024
    return pl.pallas_call(
        _mod_kernel,
        grid=(L, W // tn),
        in_specs=[pl.BlockSpec((8, D), lambda l, j: (0, 0)),
                  pl.BlockSpec((1, D, tn), lambda l, j: (l, 0, j)),
                  pl.BlockSpec((1, 1, tn), lambda l, j: (l, 0, j))],
        out_specs=pl.BlockSpec((1, 8, tn), lambda l, j: (l, 0, j)),
        out_shape=jax.ShapeDtypeStruct((L, 8, W), F32),
        compiler_params=_cparams(("parallel", "parallel")),
        name="modulation",
    )(cc, mod_w, mod_b.reshape(L, 1, W))


def _in_proj_kernel(x_ref, mt_ref, nw_ref, wb_ref, wm_ref, z_ref, zm_ref):
    x = x_ref[0]
    y = x * lax.rsqrt(jnp.mean(x * x, axis=-1, keepdims=True) + RMS_EPS) * nw_ref[...]
    h = y * (1.0 + mt_ref[0, 0, 1:2, :]) + mt_ref[0, 0, 0:1, :]
    hb = h.astype(_MXU)
    z_ref[0] = jnp.dot(hb, wb_ref[...], preferred_element_type=F32).astype(z_ref.dtype)

    @pl.when(pl.program_id(0) == 0)
    def _():
        zm_ref[0, 0] = jnp.dot(hb, wm_ref[...], preferred_element_type=F32)

    @pl.when(pl.program_id(0) != 0)
    def _():
        zm_ref[0, 0] = jnp.zeros(zm_ref.shape[2:], F32)


def _in_proj(tok, mtab, norm_w, w_big, w_misc, n_lat_tiles):
    B, T, _ = tok.shape
    D = w_big.shape[0]
    nt = T // ROW_TILE
    nj = 2
    tn = Z_WIDTH // nj
    z, zm = pl.pallas_call(
        _in_proj_kernel,
        grid=(nj, B, nt),
        in_specs=[pl.BlockSpec((1, ROW_TILE, D), lambda j, b, m: (b, m, 0)),
                  pl.BlockSpec((1, 1, 8, D), lambda j, b, m: (b, m // n_lat_tiles, 0, 0)),
                  pl.BlockSpec((1, D), lambda j, b, m: (0, 0)),
                  pl.BlockSpec((D, tn), lambda j, b, m: (0, j)),
                  pl.BlockSpec((D, LANES), lambda j, b, m: (0, 0))],
        out_specs=[pl.BlockSpec((1, ROW_TILE, tn), lambda j, b, m: (b, m, j)),
                   pl.BlockSpec((1, 1, ROW_TILE, LANES), lambda j, b, m: (j, b, m, 0))],
        out_shape=[jax.ShapeDtypeStruct((B, T, Z_WIDTH), _ACT),
                   jax.ShapeDtypeStruct((nj, B, T, LANES), F32)],
        compiler_params=_cparams(("arbitrary", "arbitrary", "arbitrary")),
        name="in_proj",
    )(tok, mtab, norm_w.reshape(1, D), w_big, w_misc)
    return z, zm[0]


def _rope(y, cos, sin, w):
    n = y.shape[-1]
    lane = _iota(y.shape, 1)
    partner = jnp.where((lane % (2 * w)) < w, pltpu.roll(y, n - w, 1), pltpu.roll(y, w, 1))
    return y * cos + partner * sin


def _attn_prep_kernel(gq_ref, gk_ref, dq_ref, dk_ref, qn_ref, kn_ref, cg_ref, sg_ref, cd_ref, sd_ref,
                      ogq_ref, ogk_ref, odq_ref, odk_ref):
    cg, sg, cd, sd = cg_ref[...], sg_ref[...], cd_ref[...], sd_ref[...]

    def head_norm(x, w):
        return x * lax.rsqrt(jnp.mean(x * x, axis=-1, keepdims=True) + RMS_EPS) * w

    for h in range(N_HEADS):
        sl = slice(h * HEAD_DIM, (h + 1) * HEAD_DIM)
        q = head_norm(gq_ref[0, :, sl].astype(F32), qn_ref[...])
        ogq_ref[0, :, sl] = (_rope(q, cg, sg, 32) * HEAD_DIM ** -0.5).astype(ogq_ref.dtype)
        dq = _rope(dq_ref[0, :, sl].astype(F32), cd, sd, 16) * DIFF_DH ** -0.5
        odq_ref[0, :, sl] = dq.astype(odq_ref.dtype)
        odk_ref[0, :, sl] = _rope(dk_ref[0, :, sl].astype(F32), cd, sd, 16).astype(odk_ref.dtype)
    for h in range(GQA_KV_HEADS):
        sl = slice(h * HEAD_DIM, (h + 1) * HEAD_DIM)
        k = head_norm(gk_ref[0, :, sl].astype(F32), kn_ref[...])
        ogk_ref[0, :, sl] = _rope(k, cg, sg, 32).astype(ogk_ref.dtype)


def _attn_prep(z, q_norm, k_norm, tabs):
    B, T, _ = z.shape
    nt = T // ROW_TILE
    zspec = lambda w, off: pl.BlockSpec((1, ROW_TILE, w), lambda b, m: (b, m, off // w))
    tspec = pl.BlockSpec((ROW_TILE, LANES), lambda b, m: (m, 0))
    wspec = pl.BlockSpec((1, LANES), lambda b, m: (0, 0))
    ospec = lambda w: pl.BlockSpec((1, ROW_TILE, w), lambda b, m: (b, m, 0))
    return pl.pallas_call(
        _attn_prep_kernel,
        grid=(B, nt),
        in_specs=[zspec(512, Z_GQA_Q), zspec(256, Z_GQA_KV), zspec(512, Z_DIFF_Q), zspec(512, Z_DIFF_K),
                  wspec, wspec, tspec, tspec, tspec, tspec],
        out_specs=[ospec(512), ospec(256), ospec(512), ospec(512)],
        out_shape=[jax.ShapeDtypeStruct((B, T, 512), _ACT), jax.ShapeDtypeStruct((B, T, 256), _ACT),
                   jax.ShapeDtypeStruct((B, T, 512), _ACT), jax.ShapeDtypeStruct((B, T, 512), _ACT)],
        compiler_params=_cparams(("parallel", "parallel")),
        name="attn_prep",
    )(z, z, z, z, q_norm.reshape(1, LANES), k_norm.reshape(1, LANES), *tabs)


def _rope_tables(n_lat):
    t = jnp.arange(n_lat)
    pos_r, pos_c = t // GRID_W, t % GRID_W

    def tab(d, reps):
        half = d // 2
        inv = ROPE_THETA ** (-jnp.arange(0, half, 2, dtype=F32) / half)
        ar = pos_r.astype(F32)[:, None] * inv
        ac = pos_c.astype(F32)[:, None] * inv
        cos = jnp.concatenate([jnp.cos(ar), jnp.cos(ar), jnp.cos(ac), jnp.cos(ac)], axis=1)
        sin = jnp.concatenate([-jnp.sin(ar), jnp.sin(ar), -jnp.sin(ac), jnp.sin(ac)], axis=1)
        cos, sin = jnp.tile(cos, (1, reps)), jnp.tile(sin, (1, reps))
        cos = jnp.concatenate([cos, jnp.ones((N_CTX, LANES), F32)], axis=0)
        sin = jnp.concatenate([sin, jnp.zeros((N_CTX, LANES), F32)], axis=0)
        return cos, sin

    cg, sg = tab(HEAD_DIM, 1)
    cd, sd = tab(DIFF_DH, 2)
    return cg, sg, cd, sd


def _softmax_step(s, m_ref, l_ref, rows):
    m_old = m_ref[rows, :]
    m_new = jnp.maximum(m_old, jnp.max(s, axis=-1, keepdims=True))
    alpha = jnp.exp(m_old - m_new)
    ps = [jnp.exp(s[:, t * LANES:(t + 1) * LANES] - m_new) for t in range(s.shape[1] // LANES)]
    psum = ps[0]
    for pt in ps[1:]:
        psum = psum + pt
    l_ref[rows, :] = alpha * l_ref[rows, :] + psum
    m_ref[rows, :] = m_new
    return jnp.concatenate(ps, axis=1), alpha


def _key_spans(k_ref, v_ref, spans, body):
    for first, tk, cnt in spans:
        def step(j, _, first=first, tk=tk):
            ks = pl.multiple_of(first + j * tk, tk)
            body(k_ref[0, pl.ds(ks, tk), :], v_ref[0, pl.ds(ks, tk), :])
            return 0
        lax.fori_loop(0, cnt, step, 0)


def _init_softmax_state(acc_ref, m_ref, l_ref):
    acc_ref[...] = jnp.zeros_like(acc_ref)
    m_ref[...] = jnp.full_like(m_ref, -jnp.inf)
    l_ref[...] = jnp.zeros_like(l_ref)


def _gqa_kernel(q_ref, k_ref, v_ref, o_ref, acc_ref, m_ref, l_ref, *, tq, spans):
    rep = N_HEADS // GQA_KV_HEADS
    q2 = jnp.concatenate([q_ref[0, :, r * HEAD_DIM:(r + 1) * HEAD_DIM] for r in range(rep)], axis=0)
    _init_softmax_state(acc_ref, m_ref, l_ref)
    allrows = slice(0, rep * tq)

    def body(kc, vc):
        p, alpha = _softmax_step(_dot_nt(q2, kc), m_ref, l_ref, allrows)
        acc_ref[...] = alpha * acc_ref[...] + _dot(p, vc)

    _key_spans(k_ref, v_ref, spans, body)
    o = acc_ref[...] / jnp.sum(l_ref[...], axis=-1, keepdims=True)
    for r in range(rep):
        o_ref[0, :, r * HEAD_DIM:(r + 1) * HEAD_DIM] = o[r * tq:(r + 1) * tq].astype(o_ref.dtype)


def _attn_scratch(rows):
    return [pltpu.VMEM((rows, HEAD_DIM), F32), pltpu.VMEM((rows, LANES), F32), pltpu.VMEM((rows, LANES), F32)]


def _gqa_attention(gq, gk, z, *, tq, q_lo, nq, spans):
    B, T, _ = gq.shape
    rep = N_HEADS // GQA_KV_HEADS
    return pl.pallas_call(
        functools.partial(_gqa_kernel, tq=tq, spans=spans),
        grid=(B, GQA_KV_HEADS, nq),
        in_specs=[pl.BlockSpec((1, tq, rep * HEAD_DIM), lambda b, g, i: (b, q_lo + i, g)),
                  pl.BlockSpec((1, T, HEAD_DIM), lambda b, g, i: (b, 0, g)),
                  pl.BlockSpec((1, T, HEAD_DIM),
                               lambda b, g, i: (b, 0, Z_GQA_KV // HEAD_DIM + GQA_KV_HEADS + g))],
        out_specs=pl.BlockSpec((1, tq, rep * HEAD_DIM), lambda b, g, i: (b, i, g)),
        out_shape=jax.ShapeDtypeStruct((B, nq * tq, GROUP_WIDTH), _ACT),
        scratch_shapes=_attn_scratch(rep * tq),
        compiler_params=_cparams(("parallel", "parallel", "parallel")),
        name="gqa_attention",
    )(gq, gk, z)


def _diff_kernel(q_ref, k_ref, v_ref, lam_ref, nw_ref, o_ref, acc_ref, m_ref, l_ref, *, tq, spans, lam_init):
    q = q_ref[0]
    lane = _iota(q.shape, 1)
    zero = jnp.zeros_like(q)
    q1 = jnp.where(lane < DIFF_DH, q, zero)
    q2 = jnp.where(lane >= DIFF_DH, q, zero)
    _init_softmax_state(acc_ref, m_ref, l_ref)
    r1, r2 = slice(0, tq), slice(tq, 2 * tq)

    def body(kc, vc):
        p1, a1 = _softmax_step(_dot_nt(q1, kc), m_ref, l_ref, r1)
        p2, a2 = _softmax_step(_dot_nt(q2, kc), m_ref, l_ref, r2)
        pv = _dot(jnp.concatenate([p1, p2], axis=0), vc)
        acc_ref[r1, :] = a1 * acc_ref[r1, :] + pv[:tq]
        acc_ref[r2, :] = a2 * acc_ref[r2, :] + pv[tq:]

    _key_spans(k_ref, v_ref, spans, body)
    lm = lam_ref[...]
    lam = (jnp.exp(jnp.sum(lm[0:1] * lm[1:2], axis=-1, keepdims=True))
           - jnp.exp(jnp.sum(lm[2:3] * lm[3:4], axis=-1, keepdims=True)) + lam_init)
    l1 = jnp.sum(l_ref[r1, :], axis=-1, keepdims=True)
    l2 = jnp.sum(l_ref[r2, :], axis=-1, keepdims=True)
    o = acc_ref[r1, :] / l1 - lam * (acc_ref[r2, :] / l2)
    y = o * lax.rsqrt(jnp.mean(o * o, axis=-1, keepdims=True) + RMS_EPS) * nw_ref[...]
    o_ref[0] = (y * (1.0 - lam_init)).astype(o_ref.dtype)


def _diff_attention(dq, dk, z, lam_p, norm_w, *, lam_init, tq, q_lo, nq, spans):
    B, T, _ = dq.shape
    return pl.pallas_call(
        functools.partial(_diff_kernel, tq=tq, spans=spans, lam_init=lam_init),
        grid=(B, N_HEADS, nq),
        in_specs=[pl.BlockSpec((1, tq, HEAD_DIM), lambda b, h, i: (b, q_lo + i, h)),
                  pl.BlockSpec((1, T, HEAD_DIM), lambda b, h, i: (b, 0, h)),
                  pl.BlockSpec((1, T, HEAD_DIM), lambda b, h, i: (b, 0, Z_DIFF_V // HEAD_DIM + h)),
                  pl.BlockSpec((8, LANES), lambda b, h, i: (0, 0)),
                  pl.BlockSpec((1, LANES), lambda b, h, i: (0, 0))],
        out_specs=pl.BlockSpec((1, tq, HEAD_DIM), lambda b, h, i: (b, i, h)),
        out_shape=jax.ShapeDtypeStruct((B, nq * tq, GROUP_WIDTH), _ACT),
        scratch_shapes=_attn_scratch(2 * tq),
        compiler_params=_cparams(("parallel", "parallel", "parallel")),
        name="diff_attention",
    )(dq, dk, z, lam_p, norm_w.reshape(1, LANES))


def _log_sigmoid(x):
    return jnp.minimum(x, 0.0) - jnp.log(1.0 + jnp.exp(-jnp.abs(x)))


def _softplus(x):
    return jnp.maximum(x, 0.0) + jnp.log(1.0 + jnp.exp(-jnp.abs(x)))


def _tri(n, kind, reps=1):
    assert n & (n - 1) == 0
    ii, jj = _iota((n, reps * n), 0), jnp.bitwise_and(_iota((n, reps * n), 1), n - 1)
    return {"le": ii >= jj, "lt": ii > jj, "ue": ii <= jj, "ut": ii < jj}[kind]


def _tri2(n):
    return jnp.concatenate([_tri(n, "le", 3), _tri(n, "ue", 3)], axis=0).astype(F32)


def _chain_orders(n_lat, n_ctx):
    fwd = [(n_lat, n_ctx, 1), (0, n_lat, 1)]
    rev = [(n_lat + n_ctx - 1, n_ctx, -1), (n_lat - 1, n_lat, -1)]
    return fwd, rev


def _head_post(o, nw, g):
    y = o * lax.rsqrt(jnp.mean(o * o, axis=-1, keepdims=True) + RMS_EPS) * nw
    return y * _silu(g)


def _gla_kernel(qk_ref, v_ref, g_ref, zm_ref, gw_ref, gb_ref, nw_ref, o_ref,
                u_ref, dl_ref, oi_ref, qg_ref, *, n_lat, n_ctx):
    C = CHUNK
    nch = n_lat + n_ctx
    le, ue = _tri(C, "le"), _tri(C, "ue")
    tri2 = _tri2(C)
    first = _iota((C, LANES), 1) < GLA_DK
    ones = jnp.ones((C, LANES), F32)
    gw, gb = gw_ref[0], gb_ref[0]
    qscale = GLA_DK ** -0.5

    def phase1(n, _):
        rows = pl.ds(pl.multiple_of(n * C, C), C)
        qk = qk_ref[0, rows, :].astype(F32)
        v = v_ref[0, rows, :]
        la = _log_sigmoid(_dot_hi(zm_ref[0, rows, :], gw) + gb) * (1.0 / GLA_TAU)
        pre, suf = _prefix_suffix(la, tri2)
        b = jnp.where(first, pre, suf)
        btot = jnp.where(first[0:1], b[C - 1:C, :], b[0:1, :])
        ref = b[C // 2:C // 2 + 1, :]
        qksw = pltpu.roll(qk, GLA_DK, 1)
        e1, e2 = jnp.exp(b - ref), jnp.exp(ref - b)
        zero = jnp.zeros_like(qk)
        qf = jnp.where(first, qk * e1, zero) * qscale
        kf = jnp.where(first, qksw * e2, zero)
        qr = jnp.where(first, zero, qksw * e1) * qscale
        kr = jnp.where(first, zero, qk * e2)
        a = jnp.where(le, _dot_nt(qf, kf), 0.0) + jnp.where(ue, _dot_nt(qr, kr), 0.0)
        oi_ref[rows, :] = _dot(a, v)
        qg_ref[rows, :] = jnp.where(first, qk, qksw) * jnp.exp(b) * qscale
        kd = jnp.where(first, qksw, qk) * jnp.exp(btot - b)
        u_ref[n] = _dot_tn(kd, v)
        la0, la1, la2 = _split(la, 3)
        dl_ref[n] = jnp.exp(_dot_tn(la0, ones) + (_dot_tn(la1, ones) + _dot_tn(la2, ones)))
        return 0

    lax.fori_loop(0, nch, phase1, 0, unroll=CHUNK_UNROLL)

    fwd, rev = _chain_orders(n_lat, n_ctx)
    hf, hr = slice(0, GLA_DK), slice(GLA_DK, 2 * GLA_DK)
    state = (jnp.zeros((GLA_DK, LANES), F32), jnp.zeros((GLA_DK, LANES), F32))
    for (f0, cnt, _), (r0, _, _) in zip(fwd, rev):
        def step(t, st, f0=f0, r0=r0):
            sf, sr = st
            cf, cr = f0 + t, r0 - t
            uf, df = u_ref[cf, hf, :], dl_ref[cf, hf, :]
            ur, dr = u_ref[cr, hr, :], dl_ref[cr, hr, :]
            u_ref[cf, hf, :] = sf
            u_ref[cr, hr, :] = sr
            return df * sf + uf, dr * sr + ur
        state = lax.fori_loop(0, cnt, step, state)

    def phase3(n, _):
        rows = pl.ds(pl.multiple_of(n * C, C), C)
        o = oi_ref[rows, :] + _dot(qg_ref[rows, :], u_ref[n])
        o_ref[0, rows, :] = _head_post(o, nw_ref[...], g_ref[0, rows, :].astype(F32)).astype(o_ref.dtype)
        return 0

    lax.fori_loop(0, nch, phase3, 0, unroll=CHUNK_UNROLL)


def _gla_mixer(z, zm, gw, gb, norm_w, n_lat_rows):
    B, T, _ = z.shape
    n_lat, n_ctx = n_lat_rows // CHUNK, (T - n_lat_rows) // CHUNK
    nch = n_lat + n_ctx
    zspec = lambda off: pl.BlockSpec((1, T, LANES), lambda b, h: (b, 0, off // LANES + h))
    return pl.pallas_call(
        functools.partial(_gla_kernel, n_lat=n_lat, n_ctx=n_ctx),
        grid=(B, N_HEADS),
        in_specs=[zspec(Z_GLA_QK), zspec(Z_GLA_V), zspec(Z_GLA_G),
                  pl.BlockSpec((1, T, LANES), lambda b, h: (b, 0, 0)),
                  pl.BlockSpec((1, LANES, LANES), lambda b, h: (h, 0, 0)),
                  pl.BlockSpec((1, 1, LANES), lambda b, h: (h, 0, 0)),
                  pl.BlockSpec((1, LANES), lambda b, h: (0, 0))],
        out_specs=pl.BlockSpec((1, T, LANES), lambda b, h: (b, 0, h)),
        out_shape=jax.ShapeDtypeStruct((B, T, GROUP_WIDTH), _ACT),
        scratch_shapes=[pltpu.VMEM((nch, LANES, LANES), F32), pltpu.VMEM((nch, LANES, LANES), F32),
                        pltpu.VMEM((T, LANES), F32), pltpu.VMEM((T, LANES), F32)],
        compiler_params=_cparams(("parallel", "parallel")),
        name="gla_mixer",
    )(z, z, z, zm, gw, gb, norm_w.reshape(1, LANES))


def _dn_gate_consts(C, h):
    trilu = jnp.concatenate([_tri(C, "le"), _tri(C, "ue")], axis=0).astype(_MXU)
    ri = _iota((8, 3 * LANES), 0)
    li = jnp.bitwise_and(_iota((8, 3 * LANES), 1), LANES - 1)
    sel3 = (((li == M_DNA + h) & (ri == 0)) | ((li == M_DNA + N_HEADS + h) & (ri == 1))).astype(_MXU)
    return _tri2(C).astype(_MXU), trilu, sel3


def _dn_gates(zm, gp, h, consts):
    C = zm.shape[0]
    lane = _iota(zm.shape, 1)
    first = lane < C
    tri2, trilu, sel3 = consts
    a_rows = _dot_nt(sel3, jnp.concatenate(_split(zm, 3), axis=1))
    pick = lambda off, d: jnp.sum(jnp.where(lane == off + d * N_HEADS + h, zm, 0.0), axis=-1, keepdims=True)
    g_cols, g_rows = [], []
    for d in range(2):
        na = -jnp.exp(gp[d:d + 1, :])
        dt = gp[2 + d:3 + d, :]
        g_cols.append(na * _softplus(pick(M_DNA, d) + dt))
        g_rows.append(na[:, :C] * _softplus(a_rows[d:d + 1, :] + dt[:, :C]))
    pre, suf = _prefix_suffix(jnp.where(first, g_cols[0], g_cols[1]), tri2)
    gc_cols = (jnp.where(first, pre, pltpu.roll(pre, C, 1)), jnp.where(first, pltpu.roll(suf, C, 1), suf))
    rsel = _iota((8, C), 0)
    stack = jnp.zeros((8, C), F32)
    for d in range(2):
        for i, piece in enumerate(_split(g_rows[d], 3)):
            stack = jnp.where(rsel == 3 * d + i, piece.astype(F32), stack)
    rc = _dot_nt(stack, trilu)
    gc_rows = (rc[0:1, :C] + rc[1:2, :C] + rc[2:3, :C],
               pltpu.roll(rc[3:4, :] + rc[4:5, :] + rc[5:6, :], C, 1)[:, :C])
    return [(jax.nn.sigmoid(pick(M_DNB, d)), gc_cols[d], gc_rows[d]) for d in range(2)]


def _dn_decay(gc_col, gc_row, mask):
    C = gc_row.shape[1]
    diff = jnp.where(mask, gc_col[:, :C] - gc_row, 0.0)
    return jnp.where(mask, jnp.exp(diff), 0.0)


def _dn_prep_kernel(qkv_ref, zm_ref, cw_ref, gp_ref, qkvn_ref, l_ref, xp_ref, *, n_lat_rows):
    C = CHUNK
    T = qkv_ref.shape[1]
    nch = T // C
    PAD = 8
    W = 3 * HEAD_DIM
    h = pl.program_id(1)
    xp_ref[0:PAD, :] = jnp.zeros((PAD, W), F32)
    xp_ref[PAD + T:PAD + T + PAD, :] = jnp.zeros((PAD, W), F32)

    def fill(n, _):
        rows = pl.ds(pl.multiple_of(n * C, C), C)
        xp_ref[pl.ds(pl.multiple_of(PAD + n * C, 8), C), :] = qkv_ref[0, rows, :].astype(F32)
        return 0

    lax.fori_loop(0, nch, fill, 0)
    cw = cw_ref[0]
    gp = gp_ref[0]
    lt, ut = _tri(C, "lt"), _tri(C, "ut")
    consts = _dn_gate_consts(C, h)
    tcol = _iota((C, 1), 0)
    half = SHORT_CONV // 2

    def chunk(n, _):
        r0 = pl.multiple_of(n * C, C)
        rows = pl.ds(r0, C)
        win = xp_ref[pl.ds(r0, C + 2 * PAD), :]
        seg = r0 >= n_lat_rows
        acc = jnp.zeros((C, W), F32)
        for i in range(SHORT_CONV):
            xs = win[PAD + i - half:PAD + i - half + C, :]
            if i != half:
                same = ((r0 + tcol + (i - half)) >= n_lat_rows) == seg
                xs = jnp.where(same, xs, 0.0)
            acc = acc + cw[i:i + 1, :] * xs
        y = _silu(acc)
        q, k, v = y[:, :HEAD_DIM], y[:, HEAD_DIM:2 * HEAD_DIM], y[:, 2 * HEAD_DIM:]
        qn = q * lax.rsqrt(jnp.sum(q * q, axis=-1, keepdims=True) + RMS_EPS) * HEAD_DIM ** -0.5
        kn = k * lax.rsqrt(jnp.sum(k * k, axis=-1, keepdims=True) + RMS_EPS)
        qkvn_ref[0, 0, rows, :] = jnp.concatenate([qn, kn, v], axis=1).astype(qkvn_ref.dtype)
        kk = _dot_nt(kn, kn)
        (bf, gcf, grf), (br, gcr, grr) = _dn_gates(zm_ref[0, rows, :], gp, h, consts)
        l_ref[0, 0, 0, n] = kk * bf * _dn_decay(gcf, grf, lt)
        l_ref[1, 0, 0, n] = kk * br * _dn_decay(gcr, grr, ut)
        return 0

    lax.fori_loop(0, nch, chunk, 0, unroll=CHUNK_UNROLL)


def _dn_prep(z, zm, conv_w, gp, n_lat_rows):
    B, T, _ = z.shape
    nch = T // CHUNK
    W = 3 * HEAD_DIM
    return pl.pallas_call(
        functools.partial(_dn_prep_kernel, n_lat_rows=n_lat_rows),
        grid=(B, N_HEADS),
        in_specs=[pl.BlockSpec((1, T, W), lambda b, h: (b, 0, Z_DN_QKV // W + h)),
                  pl.BlockSpec((1, T, LANES), lambda b, h: (b, 0, 0)),
                  pl.BlockSpec((1, 8, W), lambda b, h: (h, 0, 0)),
                  pl.BlockSpec((1, 8, LANES), lambda b, h: (h, 0, 0))],
        out_specs=[pl.BlockSpec((1, 1, T, W), lambda b, h: (b, h, 0, 0)),
                   pl.BlockSpec((2, 1, 1, nch, CHUNK, CHUNK), lambda b, h: (0, b, h, 0, 0, 0))],
        out_shape=[jax.ShapeDtypeStruct((B, N_HEADS, T, W), _ACT),
                   jax.ShapeDtypeStruct((2, B, N_HEADS, nch, CHUNK, CHUNK), F32)],
        scratch_shapes=[pltpu.VMEM((T + 16, W), F32)],
        compiler_params=_cparams(("parallel", "parallel")),
        name="dn_prep",
    )(z, zm, conv_w, gp)


def _tri_solve_kernel(l_ref, t_ref, *, n_lower):
    C = CHUNK
    t_ref[...] = jnp.zeros_like(t_ref)
    cidx = _iota((C, LANES), 0)

    def solve_row(i, jb_lo, jb_hi):
        def blk(jb, acc):
            j0 = pl.multiple_of(jb * 8, 8)
            l8 = l_ref[0, i, pl.ds(j0, 8), :]
            for k in range(8):
                acc = acc - l8[k:k + 1, :] * t_ref[0, j0 + k]
            return acc
        t_ref[0, i] = lax.fori_loop(jb_lo, jb_hi, blk, (cidx == i).astype(F32))

    @pl.when(pl.program_id(0) < n_lower)
    def _():
        def row(i, _):
            solve_row(i, 0, (i + 7) // 8)
            return 0
        lax.fori_loop(0, C, row, 0)

    @pl.when(pl.program_id(0) >= n_lower)
    def _():
        def row(t, _):
            i = C - 1 - t
            solve_row(i, i // 8, C // 8)
            return 0
        lax.fori_loop(0, C, row, 0)


def _tri_solve(lt, n_lower):
    G = lt.shape[0]
    spec = pl.BlockSpec((1, CHUNK, CHUNK, LANES), lambda g: (g, 0, 0, 0))
    return pl.pallas_call(
        functools.partial(_tri_solve_kernel, n_lower=n_lower), grid=(G,), in_specs=[spec], out_specs=spec,
        out_shape=jax.ShapeDtypeStruct(lt.shape, F32),
        compiler_params=_cparams(("parallel",)),
        name="dn_tri_solve",
    )(lt)


def _dn_scan_kernel(qkvn_ref, t_ref, zm_ref, gp_ref, g_ref, nw_ref, o_ref,
                    mk_ref, ns_ref, dl_ref, u_ref, w_ref, qg_ref, at_ref, *, n_lat, n_ctx):
    C = CHUNK
    nch = n_lat + n_ctx
    h = pl.program_id(1)
    gp = gp_ref[0]
    masks = (_tri(C, "le"), _tri(C, "ue"))
    consts = _dn_gate_consts(C, h)
    last_row = (C - 1, 0)

    def phase1(n, _):
        rows = pl.ds(pl.multiple_of(n * C, C), C)
        x = qkvn_ref[0, 0, rows, :].astype(F32)
        qn, kn, vs = x[:, :HEAD_DIM], x[:, HEAD_DIM:2 * HEAD_DIM], x[:, 2 * HEAD_DIM:]
        qk = _dot_nt(qn, kn)
        gates = _dn_gates(zm_ref[0, rows, :], gp, h, consts)
        for d in range(2):
            beta, gc, gr = gates[d]
            egc = jnp.exp(gc)
            uw = _dot(t_ref[d, 0, 0, n], jnp.concatenate([vs * beta, kn * beta * egc], axis=1))
            u, w = uw[:, :HEAD_DIM], uw[:, HEAD_DIM:]
            glast = gc[last_row[d]:last_row[d] + 1, :]
            kd = kn * jnp.exp(glast - gc)
            mn = _dot_tn(kd, uw)
            mk_ref[d, n] = (-mn[:, HEAD_DIM:]).astype(mk_ref.dtype)
            ns_ref[d, n] = mn[:, :HEAD_DIM]
            dl_ref[d, n] = jnp.broadcast_to(jnp.exp(glast), (8, LANES))
            u_ref[d, rows, :] = u
            w_ref[d, rows, :] = w.astype(w_ref.dtype)
            qg_ref[d, rows, :] = (qn * egc).astype(qg_ref.dtype)
            at_ref[d, n] = (qk * _dn_decay(gc, gr, masks[d])).astype(at_ref.dtype)
        return 0

    lax.fori_loop(0, nch, phase1, 0, unroll=CHUNK_UNROLL)

    fwd, rev = _chain_orders(n_lat, n_ctx)
    state = (jnp.zeros((HEAD_DIM, HEAD_DIM), F32), jnp.zeros((HEAD_DIM, HEAD_DIM), F32))
    for (f0, cnt, _), (r0, _, _) in zip(fwd, rev):
        def step(t, st, f0=f0, r0=r0):
            new = []
            for d, c in ((0, f0 + t), (1, r0 - t)):
                s = st[d]
                nn = ns_ref[d, c]
                ns_ref[d, c] = s
                new.append(dl_ref[d, c, 0:1, :] * s + _dot(mk_ref[d, c], s) + nn)
            return tuple(new)
        state = lax.fori_loop(0, cnt, step, state)

    def phase3(n, _):
        rows = pl.ds(pl.multiple_of(n * C, C), C)
        o = jnp.zeros((C, HEAD_DIM), F32)
        for d in range(2):
            s = ns_ref[d, n]
            ws = _dot(jnp.concatenate([w_ref[d, rows, :], qg_ref[d, rows, :]], axis=0), s)
            v_new = u_ref[d, rows, :] - ws[:C]
            o = o + ws[C:] + _dot(at_ref[d, n], v_new)
        o_ref[0, rows, :] = _head_post(o, nw_ref[...], g_ref[0, rows, :].astype(F32)).astype(o_ref.dtype)
        return 0

    lax.fori_loop(0, nch, phase3, 0, unroll=CHUNK_UNROLL)


def _dn_scan(qkvn, tmat, z, zm, gp, norm_w, n_lat_rows):
    B, T, _ = z.shape
    n_lat, n_ctx = n_lat_rows // CHUNK, (T - n_lat_rows) // CHUNK
    nch = n_lat + n_ctx
    W = 3 * HEAD_DIM
    return pl.pallas_call(
        functools.partial(_dn_scan_kernel, n_lat=n_lat, n_ctx=n_ctx),
        grid=(B, N_HEADS),
        in_specs=[pl.BlockSpec((1, 1, T, W), lambda b, h: (b, h, 0, 0)),
                  pl.BlockSpec((2, 1, 1, nch, CHUNK, CHUNK), lambda b, h: (0, b, h, 0, 0, 0)),
                  pl.BlockSpec((1, T, LANES), lambda b, h: (b, 0, 0)),
                  pl.BlockSpec((1, 8, LANES), lambda b, h: (h, 0, 0)),
                  pl.BlockSpec((1, T, LANES), lambda b, h: (b, 0, Z_DN_G // LANES + h)),
                  pl.BlockSpec((1, LANES), lambda b, h: (0, 0))],
        out_specs=pl.BlockSpec((1, T, LANES), lambda b, h: (b, 0, h)),
        out_shape=jax.ShapeDtypeStruct((B, T, GROUP_WIDTH), _ACT),
        scratch_shapes=[pltpu.VMEM((2, nch, HEAD_DIM, HEAD_DIM), _MXU),
                        pltpu.VMEM((2, nch, HEAD_DIM, HEAD_DIM), F32),
                        pltpu.VMEM((2, nch, 8, LANES), F32),
                        pltpu.VMEM((2, T, HEAD_DIM), F32),
                        pltpu.VMEM((2, T, HEAD_DIM), _MXU),
                        pltpu.VMEM((2, T, HEAD_DIM), _MXU),
                        pltpu.VMEM((2, nch, CHUNK, CHUNK), _MXU)],
        compiler_params=_cparams(("parallel", "parallel")),
        name="dn_scan",
    )(qkvn, tmat, zm, gp, z, norm_w.reshape(1, LANES))


def _dn_mixer(z, zm, conv_w, gp, norm_w, n_lat_rows):
    B, T, _ = z.shape
    qkvn, lmat = _dn_prep(z, zm, conv_w, gp, n_lat_rows)
    n_sys = B * N_HEADS * (T // CHUNK)
    n_grp = pl.cdiv(n_sys, LANES)
    flat = lmat.reshape(2, n_sys, CHUNK, CHUNK)
    flat = jnp.pad(flat, ((0, 0), (0, n_grp * LANES - n_sys), (0, 0), (0, 0)))
    sol = _tri_solve(flat.reshape(2 * n_grp, LANES, CHUNK, CHUNK).transpose(0, 2, 3, 1), n_grp)
    sol = sol.transpose(0, 3, 1, 2).reshape(2, n_grp * LANES, CHUNK, CHUNK)[:, :n_sys]
    tmat = sol.reshape(lmat.shape)
    return _dn_scan(qkvn, tmat, z, zm, gp, norm_w, n_lat_rows)


def _out_proj_kernel(*refs, n_lat_tiles, with_ctx):
    n_in = 4 + (2 if with_ctx else 0)
    parts = list(refs[:4])
    ctx_parts = refs[4:n_in]
    w_ref, x_ref, mt_ref, nw_ref, rw_ref, xo_ref, at_ref = refs[n_in:]
    D = x_ref.shape[2]
    is_ctx = pl.program_id(1) >= n_lat_tiles
    o = None
    for g, p_ref in enumerate(parts):
        p = p_ref[0]
        if with_ctx and g >= 2:
            p = jnp.where(is_ctx, ctx_parts[g - 2][0], p)
        t = jnp.dot(p, w_ref[g * GROUP_WIDTH:(g + 1) * GROUP_WIDTH, :], preferred_element_type=F32)
        o = t if o is None else o + t
    xn = x_ref[0] + mt_ref[0, 0, 2:3, :] * o
    xo_ref[0, :, 0:D] = xn
    y = xn * lax.rsqrt(jnp.mean(xn * xn, axis=-1, keepdims=True) + RMS_EPS) * nw_ref[...]
    h2 = y * (1.0 + mt_ref[0, 0, 4:5, :]) + mt_ref[0, 0, 3:4, :]
    logits = _dot_hi(h2, rw_ref[...])
    valid = _iota(logits.shape, 1) < N_EXPERTS
    lg = jnp.where(valid, logits, -jnp.inf)
    e = jnp.exp(lg - jnp.max(lg, axis=-1, keepdims=True))
    aff = e / jnp.sum(e, axis=-1, keepdims=True)
    xo_ref[0, :, D:2 * D] = h2
    xo_ref[0, :, 2 * D:] = aff
    at_ref[0] = aff.T[:N_EXPERTS, :]


def _out_proj(parts, ctx_parts, w_out, tok, mtab, norm_w, router_w, n_lat_tiles):
    B, T, _ = tok.shape
    D = w_out.shape[0]
    with_ctx = ctx_parts is not None
    n_tiles = T // ROW_TILE if with_ctx else n_lat_tiles
    rows = n_tiles * ROW_TILE
    full = pl.BlockSpec((1, ROW_TILE, GROUP_WIDTH), lambda b, m: (b, m, 0))
    lat = pl.BlockSpec((1, ROW_TILE, GROUP_WIDTH), lambda b, m: (b, jnp.minimum(m, n_lat_tiles - 1), 0))
    ctx = pl.BlockSpec((1, ROW_TILE, GROUP_WIDTH), lambda b, m: (b, jnp.maximum(m - n_lat_tiles, 0), 0))
    xspec = pl.BlockSpec((1, ROW_TILE, D), lambda b, m: (b, m, 0))
    in_specs = [full, full, lat, lat] + ([ctx, ctx] if with_ctx else [])
    args = list(parts) + (list(ctx_parts) if with_ctx else [])
    return pl.pallas_call(
        functools.partial(_out_proj_kernel, n_lat_tiles=n_lat_tiles, with_ctx=with_ctx),
        grid=(B, n_tiles),
        in_specs=in_specs + [pl.BlockSpec((D, D), lambda b, m: (0, 0)),
                             xspec,
                             pl.BlockSpec((1, 1, 8, D), lambda b, m: (b, m // n_lat_tiles, 0, 0)),
                             pl.BlockSpec((1, D), lambda b, m: (0, 0)),
                             pl.BlockSpec((D, LANES), lambda b, m: (0, 0))],
        out_specs=[pl.BlockSpec((1, ROW_TILE, 2 * D + LANES), lambda b, m: (b, m, 0)),
                   pl.BlockSpec((1, N_EXPERTS, ROW_TILE), lambda b, m: (b, 0, m))],
        out_shape=[jax.ShapeDtypeStruct((B, rows, 2 * D + LANES), F32),
                   jax.ShapeDtypeStruct((B, N_EXPERTS, rows), F32)],
        compiler_params=_cparams(("parallel", "parallel")),
        name="out_proj",
    )(*args, w_out, tok, mtab, norm_w.reshape(1, D), router_w)


def _route_kernel(at_ref, idx_ref, rank_ref, *, lo, n, cap):
    E = N_EXPERTS
    aff = at_ref[0, :, lo:lo + n]
    tok = _iota((E, n), 1)
    count = lambda m: jnp.sum(m.astype(F32), axis=-1, keepdims=True)
    capf = float(cap)

    def vbit(i, t):
        cand = t | jnp.left_shift(jnp.int32(1), 30 - i)
        return jnp.where(count(aff >= pltpu.bitcast(cand, F32)) >= capf, cand, t)

    thr = pltpu.bitcast(lax.fori_loop(0, 31, vbit, jnp.zeros((E, 1), jnp.int32)), F32)
    gt, eq = aff > thr, aff == thr
    need = capf - count(gt)

    def ibit(i, m):
        cand = m | jnp.left_shift(jnp.int32(1), 12 - i)
        return jnp.where(count(eq & (tok < cand)) < need, cand, m)

    m = lax.fori_loop(0, 13, ibit, jnp.zeros((E, 1), jnp.int32))
    sel = gt | (eq & (tok <= m))
    ustrict = _tri(LANES, "ut").astype(F32)
    carry = jnp.zeros((E, 1), F32)
    for t in range(n // LANES):
        sl = slice(t * LANES, (t + 1) * LANES)
        s = sel[:, sl].astype(F32)
        rank_ref[:, sl] = jnp.where(sel[:, sl], _dot(s, ustrict) + carry, -1.0)
        carry = carry + jnp.sum(s, axis=-1, keepdims=True)
    tokc = _iota((8, n), 1)
    rsel = _iota((8, n), 0)
    rn = jnp.where(rsel == 0, tokc // 64, jnp.where(rsel == 1, tokc % 64, 0)).astype(F32)

    def per_expert(e, _):
        rk = rank_ref[pl.ds(e, 1), :]
        onehot = (rk == _iota((cap, n), 0).astype(F32)).astype(F32)
        res = _dot_nt(rn, onehot)
        idx_ref[0, pl.ds(e, 1), :] = (res[0:1, :] * 64.0 + res[1:2, :]).astype(jnp.int32)
        return 0

    lax.fori_loop(0, E, per_expert, 0)


def _route(aff_t, lo, n, cap):
    B, E, T = aff_t.shape
    return pl.pallas_call(
        functools.partial(_route_kernel, lo=lo, n=n, cap=cap),
        grid=(B,),
        in_specs=[pl.BlockSpec((1, E, T), lambda b: (b, 0, 0))],
        out_specs=pl.BlockSpec((1, E, cap), lambda b: (b, 0, 0)),
        out_shape=jax.ShapeDtypeStruct((B, E, cap), jnp.int32),
        scratch_shapes=[pltpu.VMEM((E, n), F32)],
        compiler_params=_cparams(("parallel",)),
        name="route_topk",
    )(aff_t)


def _moe_kernel(idx_ref, comb_hbm, gt_ref, wg_ref, wu_ref, wd_ref, out_hbm, *scratch, groups):
    del comb_hbm
    e = pl.program_id(0)
    D = gt_ref.shape[1]
    bufs, sem = scratch[:MOE_BUFS], scratch[MOE_BUFS]
    sem_g = lambda g: sem.at[g % MOE_BUFS]
    sem_s = lambda g: sem.at[MOE_BUFS + g % MOE_BUFS]

    def start_gather(g):
        base, n, _ = groups[g]
        for r in range(n):
            row = pl.ds(idx_ref[e, base + r], 1)
            pltpu.make_async_copy(out_hbm.at[row, :], bufs[g % MOE_BUFS].at[pl.ds(r, 1), :], sem_g(g)).start()

    def start_scatter(g):
        base, n, _ = groups[g]
        for r in range(n):
            row = pl.ds(idx_ref[e, base + r], 1)
            pltpu.make_async_copy(bufs[g % MOE_BUFS].at[pl.ds(r, 1), pl.ds(0, D)], out_hbm.at[row, pl.ds(0, D)],
                                  sem_s(g)).start()

    def wait_gather(g):
        rows = pl.ds(0, groups[g][1])
        pltpu.make_async_copy(out_hbm.at[rows, :], bufs[g % MOE_BUFS].at[rows, :], sem_g(g)).wait()

    def wait_scatter(g):
        rows = pl.ds(0, groups[g][1])
        pltpu.make_async_copy(bufs[g % MOE_BUFS].at[rows, pl.ds(0, D)], out_hbm.at[rows, pl.ds(0, D)], sem_s(g)).wait()

    def compute(g):
        _, n, grow = groups[g]
        buf = bufs[g % MOE_BUFS]
        xs = buf[0:n, D:2 * D].astype(_MXU)
        aff = buf[0:n, 2 * D:]
        wt = jnp.sum(jnp.where(_iota(aff.shape, 1) == e, aff, 0.0), axis=-1, keepdims=True)
        hid = _silu(jnp.dot(xs, wg_ref[0], preferred_element_type=F32)) * \
            jnp.dot(xs, wu_ref[0], preferred_element_type=F32)
        y = jnp.dot(hid.astype(_MXU), wd_ref[0], preferred_element_type=F32) * wt
        buf[0:n, 0:D] = buf[0:n, 0:D] + gt_ref[grow:grow + 1, :] * y

    G = len(groups)
    for g in range(min(2, G)):
        start_gather(g)
    for g in range(G):
        wait_gather(g)
        if g >= 2:
            wait_scatter(g - 2)
        if g + 2 < G:
            start_gather(g + 2)
        if g >= 1:
            start_scatter(g - 1)
        compute(g)
    start_scatter(G - 1)
    for g in range(max(G - 2, 0), G):
        wait_scatter(g)


MOE_ROWS = 256
MOE_BUFS = 4


def _moe_groups(B, cap_lat, cap_ctx):
    groups = []
    for b in range(B):
        for off in range(0, cap_lat, MOE_ROWS):
            groups.append((b * cap_lat + off, min(MOE_ROWS, cap_lat - off), b))
    if cap_ctx:
        assert B * cap_ctx <= MOE_ROWS
        groups.append((B * cap_lat, B * cap_ctx, B))
    return tuple(groups)


def _moe(idx, comb, gtab, wg, wu, wd, groups):
    E, R = idx.shape
    BT, W = comb.shape
    D, FF = wg.shape[1], wg.shape[2]
    grid_spec = pltpu.PrefetchScalarGridSpec(
        num_scalar_prefetch=1,
        grid=(E,),
        in_specs=[pl.BlockSpec(memory_space=pl.ANY),
                  pl.BlockSpec((8, D), lambda e, idx: (0, 0)),
                  pl.BlockSpec((1, D, FF), lambda e, idx: (e, 0, 0)),
                  pl.BlockSpec((1, D, FF), lambda e, idx: (e, 0, 0)),
                  pl.BlockSpec((1, FF, D), lambda e, idx: (e, 0, 0))],
        out_specs=pl.BlockSpec(memory_space=pl.ANY),
        scratch_shapes=[pltpu.VMEM((MOE_ROWS, W), F32)] * MOE_BUFS + [pltpu.SemaphoreType.DMA((2 * MOE_BUFS,))],
    )
    return pl.pallas_call(
        functools.partial(_moe_kernel, groups=groups),
        grid_spec=grid_spec,
        out_shape=jax.ShapeDtypeStruct((BT, W), F32),
        input_output_aliases={1: 0},
        compiler_params=_cparams(("arbitrary",)),
        name="moe_experts",
    )(idx, comb, gtab, wg, wu, wd)


def _final_norm_kernel(x_ref, w_ref, o_ref):
    x = x_ref[0]
    o_ref[0] = x * lax.rsqrt(jnp.mean(x * x, axis=-1, keepdims=True) + RMS_EPS) * w_ref[...]


def _final_norm(tok, w, n_lat_rows):
    B = tok.shape[0]
    D = w.shape[0]
    spec = pl.BlockSpec((1, ROW_TILE, D), lambda b, m: (b, m, 0))
    return pl.pallas_call(
        _final_norm_kernel,
        grid=(B, n_lat_rows // ROW_TILE),
        in_specs=[spec, pl.BlockSpec((1, D), lambda b, m: (0, 0))],
        out_specs=spec,
        out_shape=jax.ShapeDtypeStruct((B, n_lat_rows, D), F32),
        compiler_params=_cparams(("parallel", "parallel")),
        name="final_norm",
    )(tok, w.reshape(1, D))


def _in_proj_columns():
    o = {}
    off = 0
    for name, w in (("gla_q", 256), ("gla_k", 256), ("gla_v", 512), ("gla_g", 512), ("gla_lr", 32),
                    ("dn_qkv", 1536), ("dn_g", 512), ("dn_a", 8), ("dn_b", 8), ("gqa_q", 512),
                    ("gqa_kv", 512), ("diff_q", 512), ("diff_k", 512), ("diff_v", 512)):
        o[name] = off
        off += w
    rng = lambda a, n: list(range(a, a + n))
    cols = []
    for h in range(N_HEADS):
        cols += rng(o["gla_q"] + GLA_DK * h, GLA_DK) + rng(o["gla_k"] + GLA_DK * h, GLA_DK)
    cols += rng(o["gla_v"], 512) + rng(o["gla_g"], 512)
    for h in range(N_HEADS):
        for part in range(3):
            cols += rng(o["dn_qkv"] + part * GROUP_WIDTH + HEAD_DIM * h, HEAD_DIM)
    cols += rng(o["dn_g"], 512) + rng(o["gqa_q"], 512) + rng(o["gqa_kv"], 512)
    cols += rng(o["diff_q"], 512) + rng(o["diff_k"], 512) + rng(o["diff_v"], 512)
    assert len(cols) == Z_WIDTH
    misc = rng(o["gla_lr"], 32) + rng(o["dn_a"], 8) + rng(o["dn_b"], 8)
    dn_cols = [c - o["dn_qkv"] for c in cols[Z_DN_QKV:Z_DN_QKV + 3 * GROUP_WIDTH]]
    return np.asarray(cols), np.asarray(misc), np.asarray(dn_cols)


def _runs(cols):
    cuts = [0] + [i for i in range(1, len(cols)) if cols[i] != cols[i - 1] + 1] + [len(cols)]
    return [(int(cols[a]), int(cols[b - 1]) + 1) for a, b in zip(cuts[:-1], cuts[1:])]


def _layer_params(l, p):
    cols, misc, dn_cols = _in_proj_columns()
    D = p["w_in"].shape[1]
    w_in = p["w_in"][l]
    w_big = jnp.concatenate([w_in[:, a:b] for a, b in _runs(cols)], axis=1).astype(_MXU)
    w_misc = jnp.pad(w_in[:, misc], ((0, 0), (0, LANES - len(misc)))).astype(_MXU)
    w2, gb2 = p["gla_gate_w2"][l], p["gla_gate_b"][l]
    gw = jnp.zeros((N_HEADS, LANES, LANES), F32)
    gb = []
    for h in range(N_HEADS):
        hs = slice(h * GLA_DK, (h + 1) * GLA_DK)
        gw = gw.at[h, 0:GLA_RANK, 0:GLA_DK].set(w2[0][:, hs])
        gw = gw.at[h, GLA_RANK:2 * GLA_RANK, GLA_DK:2 * GLA_DK].set(w2[1][:, hs])
        gb.append(jnp.concatenate([gb2[0][hs], gb2[1][hs]])[None, :])
    gb = jnp.stack(gb)
    conv = p["dn_conv_w"][l][:, dn_cols].reshape(SHORT_CONV, N_HEADS, 3 * HEAD_DIM).transpose(1, 0, 2)
    conv = jnp.pad(conv, ((0, 0), (0, 8 - SHORT_CONV), (0, 0)))
    al, dt = p["dn_a_log"][l], p["dn_dt_bias"][l]
    gp = jnp.stack([al[0], al[1], dt[0], dt[1]], axis=0).T
    gp = jnp.pad(jnp.broadcast_to(gp[:, :, None], (N_HEADS, 4, LANES)), ((0, 0), (0, 4), (0, 0)))
    lam_p = jnp.pad(p["diff_lambda"][l], ((0, 4), (0, LANES - DIFF_DH)))
    router = jnp.pad(p["router_w"][l], ((0, 0), (0, LANES - N_EXPERTS)))
    return dict(w_big=w_big, w_misc=w_misc, gw=gw, gb=gb, conv=conv, gp=gp, lam_p=lam_p, router=router,
                w_out=p["w_out"][l].astype(_MXU),
                wg=p["exp_w_gate"][l].astype(_MXU), wu=p["exp_w_up"][l].astype(_MXU),
                wd=p["exp_w_down"][l].astype(_MXU))


def _mixers(z, zm, lp, p, l, tabs, n_lat_rows, need_ctx, lam_init):
    T = z.shape[1]
    gla = _gla_mixer(z, zm, lp["gw"], lp["gb"], p["gla_norm_w"][l], n_lat_rows)
    dn = _dn_mixer(z, zm, lp["conv"], lp["gp"], p["dn_norm_w"][l], n_lat_rows)
    gq, gk, dq, dk = _attn_prep(z, p["gqa_q_norm"][l], p["gqa_k_norm"][l], tabs)
    tq = 4 * ROW_TILE
    ctx_span = (n_lat_rows, ROW_TILE, (T - n_lat_rows) // ROW_TILE)
    lat = dict(tq=tq, q_lo=0, nq=n_lat_rows // tq, spans=((0, tq, n_lat_rows // tq), ctx_span))
    ctx = dict(tq=ROW_TILE, q_lo=n_lat_rows // ROW_TILE, nq=(T - n_lat_rows) // ROW_TILE, spans=(ctx_span,))
    diff_args = (dq, dk, z, lp["lam_p"], p["diff_norm_w"][l])
    parts = (gla, dn, _gqa_attention(gq, gk, z, **lat), _diff_attention(*diff_args, lam_init=lam_init, **lat))
    ctx_parts = None
    if need_ctx:
        ctx_parts = (_gqa_attention(gq, gk, z, **ctx), _diff_attention(*diff_args, lam_init=lam_init, **ctx))
    return parts, ctx_parts


def _forward(x, c, ctx, c_ctx, p):
    B, N, D = x.shape
    n_ctx = ctx.shape[1]
    T = N + n_ctx
    L = p["mod_w"].shape[0]
    n_lat_tiles = N // ROW_TILE
    cc = jnp.concatenate([c, c_ctx[None, :], jnp.zeros((8 - B - 1, D), F32)], axis=0)
    mods = _modulation(cc, p["mod_w"], p["mod_b"])
    tok = jnp.concatenate([x, ctx], axis=1)
    tabs = _rope_tables(N)
    cap_lat = EC_CAPACITY * N // N_EXPERTS
    cap_ctx = EC_CAPACITY * n_ctx // N_EXPERTS
    for l in range(L):
        last = l == L - 1
        lam_init = 0.8 - 0.6 * math.exp(-0.3 * l)
        lp = _layer_params(l, p)
        m6 = mods[l].reshape(8, 6, D)
        mtab = jnp.stack([m6[:B], jnp.broadcast_to(m6[B:B + 1], (B, 6, D))], axis=1)
        mtab = jnp.pad(mtab, ((0, 0), (0, 0), (0, 2), (0, 0)))
        z, zm = _in_proj(tok, mtab, p["norm1_w"][l], lp["w_big"], lp["w_misc"], n_lat_tiles)
        parts, ctx_parts = _mixers(z, zm, lp, p, l, tabs, N, not last, lam_init)
        tok, aff_t = _out_proj(parts, ctx_parts, lp["w_out"], tok, mtab, p["norm2_w"][l], lp["router"],
                               n_lat_tiles)
        rows_out, W = tok.shape[1:]
        base = (jnp.arange(B, dtype=jnp.int32) * rows_out)[None, :, None]
        idx = (_route(aff_t, 0, N, cap_lat).transpose(1, 0, 2) + base).reshape(N_EXPERTS, B * cap_lat)
        if not last:
            ic = _route(aff_t, N, n_ctx, cap_ctx).transpose(1, 0, 2) + base + N
            idx = jnp.concatenate([idx, ic.reshape(N_EXPERTS, B * cap_ctx)], axis=1)
        groups = _moe_groups(B, cap_lat, 0 if last else cap_ctx)
        gtab = jnp.pad(m6[:B + 1, 5, :], ((0, 8 - B - 1), (0, 0)))
        tok = _moe(idx, tok.reshape(B * rows_out, W), gtab, lp["wg"], lp["wu"], lp["wd"],
                   groups).reshape(B, rows_out, W)
    return _final_norm(tok, p["final_norm_w"], N)


def kernel(x, c, ctx, c_ctx, mod_w, mod_b, norm1_w, norm2_w, w_in, w_out, gla_gate_w2, gla_gate_b, gla_norm_w,
           dn_conv_w, dn_a_log, dn_dt_bias, dn_norm_w, gqa_q_norm, gqa_k_norm, diff_lambda, diff_norm_w,
           router_w, exp_w_gate, exp_w_up, exp_w_down, final_norm_w):
    p = dict(mod_w=mod_w, mod_b=mod_b, norm1_w=norm1_w, norm2_w=norm2_w, w_in=w_in, w_out=w_out,
             gla_gate_w2=gla_gate_w2, gla_gate_b=gla_gate_b, gla_norm_w=gla_norm_w, dn_conv_w=dn_conv_w,
             dn_a_log=dn_a_log, dn_dt_bias=dn_dt_bias, dn_norm_w=dn_norm_w, gqa_q_norm=gqa_q_norm,
             gqa_k_norm=gqa_k_norm, diff_lambda=diff_lambda, diff_norm_w=diff_norm_w, router_w=router_w,
             exp_w_gate=exp_w_gate, exp_w_up=exp_w_up, exp_w_down=exp_w_down, final_norm_w=final_norm_w)
    return _forward(x, c, ctx, c_ctx, p)
```
